```python
import math
import jax, jax.numpy as jnp
from jax import lax
import numpy as np

D_MODEL = 1024
BATCH = 32
SEQ = 2048
DEPTH = 1
DEC_BATCH = 8
DEC_SEQ = 2048
PAST_LEN = 128

RET_HEADS = 4
RET_V_DIM = D_MODEL // RET_HEADS
RET_QK_DIM = RET_V_DIM // 2
RET_CHUNK = 128
ROPE_BASE = 10000.0
DIFF_HEADS = 8
DIFF_V_DIM = D_MODEL // DIFF_HEADS
DIFF_SUB_DIM = DIFF_V_DIM // 2
Q_BLOCK = 128
N_BUCKETS = 32
MAX_DISTANCE = 128
D_FF = ((8 * D_MODEL // 3 + 255) // 256) * 256
EPS = 1e-6

COLS = (RET_HEADS * RET_QK_DIM, RET_HEADS * RET_QK_DIM, RET_HEADS * RET_V_DIM, RET_HEADS * RET_V_DIM,
        DIFF_HEADS * 2 * DIFF_SUB_DIM, DIFF_HEADS * 2 * DIFF_SUB_DIM, DIFF_HEADS * DIFF_V_DIM, 2 * D_MODEL)
IN_COLS = sum(COLS)
SPLITS = tuple(int(s) for s in np.cumsum(COLS)[:-1])

kernel_name = "hybrid_retention_diffattn_encoder"


def rms_norm(x, g):
    xf = x.astype(jnp.float32)
    y = xf * lax.rsqrt(jnp.mean(xf * xf, axis=-1, keepdims=True) + EPS)
    return (y * g.astype(jnp.float32)).astype(x.dtype)


def rms_unit(xf):
    return xf * lax.rsqrt(jnp.mean(xf * xf, axis=-1, keepdims=True) + EPS)


def rotary(x):
    S, d = x.shape[1], x.shape[-1]
    half = d // 2
    inv = ROPE_BASE ** (-jnp.arange(half, dtype=jnp.float32) / half)
    ang = jnp.arange(S, dtype=jnp.float32)[:, None] * inv[None, :]
    cos = jnp.cos(ang)[None, :, None, :]
    sin = jnp.sin(ang)[None, :, None, :]
    x1, x2 = x[..., :half], x[..., half:]
    return jnp.concatenate([x1 * cos - x2 * sin, x1 * sin + x2 * cos], axis=-1)


def retention_dir(q, k, v, log_gamma, inclusive):
    B, S, H, dk = q.shape
    dv = v.shape[-1]
    C = RET_CHUNK
    N = S // C
    to_chunks = lambda t: t.reshape(B, N, C, H, t.shape[-1]).transpose(1, 0, 3, 2, 4)
    qc, kc, vc = to_chunks(q), to_chunks(k), to_chunks(v)
    pos = jnp.arange(C, dtype=jnp.float32)
    diff = pos[:, None] - pos[None, :]
    mask = (diff >= 0) if inclusive else (diff > 0)
    lg = log_gamma[:, None, None]
    dmat = jnp.where(mask[None], jnp.exp(lg * jnp.maximum(diff, 0.0)[None]), 0.0)
    xi = jnp.exp(log_gamma[:, None] * (pos + 1.0)[None, :])[None, :, :, None]
    zeta = jnp.exp(log_gamma[:, None] * (C - 1.0 - pos)[None, :])[None, :, :, None]
    chunk_decay = jnp.exp(log_gamma * C)[None, :, None, None]

    def step(state, inp):
        qi, ki, vi = inp
        scores = jnp.einsum('bhik,bhjk->bhij', qi, ki) * dmat[None]
        intra = jnp.einsum('bhij,bhjv->bhiv', scores, vi)
        inter = jnp.einsum('bhik,bhkv->bhiv', qi, state) * xi
        new_state = state * chunk_decay + jnp.einsum('bhjk,bhjv->bhkv', ki * zeta, vi)
        return new_state, intra + inter

    state0 = jnp.zeros((B, H, dk, dv), jnp.float32)
    _, out = lax.scan(step, state0, (qc, kc, vc))
    return out.transpose(1, 0, 3, 2, 4).reshape(B, S, H, dv)


def t5_bucket(rel):
    nb = N_BUCKETS // 2
    max_exact = nb // 2
    ret = jnp.where(rel > 0, nb, 0)
    n = jnp.abs(rel)
    nf = jnp.maximum(n, 1).astype(jnp.float32)
    large = max_exact + (jnp.log(nf / max_exact) / math.log(MAX_DISTANCE / max_exact)
                         * (nb - max_exact)).astype(jnp.int32)
    large = jnp.minimum(large, nb - 1)
    return ret + jnp.where(n < max_exact, n, large)


def diff_attention(q, k, v, lam, table):
    B, S, H, _, dh = q.shape
    NB = S // Q_BLOCK
    qb = q.reshape(B, NB, Q_BLOCK, H, 2, dh).transpose(1, 0, 2, 3, 4, 5)
    starts = jnp.arange(NB, dtype=jnp.int32) * Q_BLOCK
    kpos = jnp.arange(S, dtype=jnp.int32)

    def block(args):
        qblk, start = args
        qpos = start + jnp.arange(Q_BLOCK, dtype=jnp.int32)
        bias = table[t5_bucket(kpos[None, :] - qpos[:, None])].transpose(2, 0, 1)
        logits = jnp.einsum('bqhcd,bkhcd->bhcqk', qblk, k).astype(jnp.float32) + bias[None, :, None].astype(jnp.float32)
        p = jax.nn.softmax(logits, axis=-1)
        attn = p[:, :, 0] - lam * p[:, :, 1]
        return jnp.einsum('bhqk,bkhv->bqhv', attn, v)

    out = lax.map(block, (qb, starts))
    return out.transpose(1, 0, 2, 3, 4).reshape(B, S, H, v.shape[-1])


def encoder_layer(x, layer, rel_bias_table, norm_mix_g, w_in, ret_decay_fwd, ret_decay_bwd,
                  q_norm_g, k_norm_g, lam_q1, lam_k1, lam_q2, lam_k2, subln_g, w_out,
                  norm_ffn_g, w_gate, w_up, w_down):
    B, S, _ = x.shape
    f32 = jnp.float32
    h = rms_norm(x, norm_mix_g)
    proj = h @ w_in
    rq, rk, rv, rg, dq, dk, dv, mg = jnp.split(proj, SPLITS, axis=-1)

    rq = rotary(rq.reshape(B, S, RET_HEADS, RET_QK_DIM).astype(f32))
    rk = rotary(rk.reshape(B, S, RET_HEADS, RET_QK_DIM).astype(f32)) * (RET_QK_DIM ** -0.5)
    rv = rv.reshape(B, S, RET_HEADS, RET_V_DIM).astype(f32)
    log_gf = jnp.log1p(-jnp.exp(ret_decay_fwd.astype(f32)))
    log_gb = jnp.log1p(-jnp.exp(ret_decay_bwd.astype(f32)))
    ret_f = retention_dir(rq, rk, rv, log_gf, True)
    ret_b = retention_dir(rq[:, ::-1], rk[:, ::-1], rv[:, ::-1], log_gb, False)[:, ::-1]
    ret = rms_unit(ret_f + ret_b).reshape(B, S, D_MODEL)
    ret_out = jax.nn.silu(rg.astype(f32)) * ret

    lambda_init = 0.8 - 0.6 * math.exp(-0.3 * layer)
    dq = rms_norm(dq.reshape(B, S, DIFF_HEADS, 2, DIFF_SUB_DIM), q_norm_g) * (DIFF_SUB_DIM ** -0.5)
    dk = rms_norm(dk.reshape(B, S, DIFF_HEADS, 2, DIFF_SUB_DIM), k_norm_g)
    dv = dv.reshape(B, S, DIFF_HEADS, DIFF_V_DIM).astype(f32)
    lam = (jnp.exp(jnp.sum(lam_q1.astype(f32) * lam_k1.astype(f32)))
           - jnp.exp(jnp.sum(lam_q2.astype(f32) * lam_k2.astype(f32))) + lambda_init)
    o = diff_attention(dq, dk, dv, lam, rel_bias_table)
    diff_out = (rms_norm(o, subln_g) * (1.0 - lambda_init)).reshape(B, S, D_MODEL)

    gates = jax.nn.sigmoid(mg.astype(f32))
    merged = gates[..., :D_MODEL] * ret_out + gates[..., D_MODEL:] * diff_out
    x = x + merged.astype(x.dtype) @ w_out

    h = rms_norm(x, norm_ffn_g)
    return x + (jax.nn.silu(h @ w_gate) * (h @ w_up)) @ w_down


def trunk(x, rel_bias_table, norm_mix_g, w_in, ret_decay_fwd, ret_decay_bwd, q_norm_g, k_norm_g,
          lam_q1, lam_k1, lam_q2, lam_k2, subln_g, w_out, norm_ffn_g, w_gate, w_up, w_down):
    for l in range(DEPTH):
        x = encoder_layer(x, l, rel_bias_table, norm_mix_g[l], w_in[l], ret_decay_fwd[l], ret_decay_bwd[l],
                          q_norm_g[l], k_norm_g[l], lam_q1[l], lam_k1[l], lam_q2[l], lam_k2[l], subln_g[l],
                          w_out[l], norm_ffn_g[l], w_gate[l], w_up[l], w_down[l])
    return x


def setup_inputs(seed: int = 0) -> dict:
    key = jax.random.key(seed)
    ks = jax.random.split(key, 20)
    nrm = lambda k, shape, s: jax.random.normal(k, shape, jnp.float32) * s
    base_decay = jnp.log(jnp.exp(jnp.linspace(math.log(1.0 / 32), math.log(1.0 / 512), RET_HEADS))).astype(jnp.float32)
    return {
        "x_prompt": nrm(ks[0], (BATCH, SEQ, D_MODEL), 1.0),
        "x_sample": nrm(ks[1], (DEC_BATCH, DEC_SEQ, D_MODEL), 1.0),
        "rel_bias_table": nrm(ks[2], (N_BUCKETS, DIFF_HEADS), 0.5),
        "norm_mix_g": 1.0 + nrm(ks[3], (DEPTH, D_MODEL), 0.02),
        "w_in": nrm(ks[4], (DEPTH, D_MODEL, IN_COLS), D_MODEL ** -0.5),
        "ret_decay_fwd": base_decay[None] + nrm(ks[5], (DEPTH, RET_HEADS), 0.05),
        "ret_decay_bwd": base_decay[None] + nrm(ks[6], (DEPTH, RET_HEADS), 0.05),
        "q_norm_g": 1.0 + nrm(ks[7], (DEPTH, DIFF_SUB_DIM), 0.02),
        "k_norm_g": 1.0 + nrm(ks[8], (DEPTH, DIFF_SUB_DIM), 0.02),
        "lam_q1": nrm(ks[9], (DEPTH, DIFF_SUB_DIM), 0.1),
        "lam_k1": nrm(ks[10], (DEPTH, DIFF_SUB_DIM), 0.1),
        "lam_q2": nrm(ks[11], (DEPTH, DIFF_SUB_DIM), 0.1),
        "lam_k2": nrm(ks[12], (DEPTH, DIFF_SUB_DIM), 0.1),
        "subln_g": 1.0 + nrm(ks[13], (DEPTH, DIFF_V_DIM), 0.02),
        "w_out": nrm(ks[14], (DEPTH, D_MODEL, D_MODEL), D_MODEL ** -0.5),
        "norm_ffn_g": 1.0 + nrm(ks[15], (DEPTH, D_MODEL), 0.02),
        "w_gate": nrm(ks[16], (DEPTH, D_MODEL, D_FF), D_MODEL ** -0.5),
        "w_up": nrm(ks[17], (DEPTH, D_MODEL, D_FF), D_MODEL ** -0.5),
        "w_down": nrm(ks[18], (DEPTH, D_FF, D_MODEL), D_FF ** -0.5),
    }


def reference(x_prompt, x_sample, rel_bias_table, norm_mix_g, w_in, ret_decay_fwd, ret_decay_bwd,
              q_norm_g, k_norm_g, lam_q1, lam_k1, lam_q2, lam_k2, subln_g, w_out, norm_ffn_g,
              w_gate, w_up, w_down):
    y_prompt = trunk(x_prompt, rel_bias_table, norm_mix_g, w_in, ret_decay_fwd, ret_decay_bwd, q_norm_g,
                     k_norm_g, lam_q1, lam_k1, lam_q2, lam_k2, subln_g, w_out, norm_ffn_g, w_gate, w_up, w_down)
    y_sample = trunk(x_sample, rel_bias_table, norm_mix_g, w_in, ret_decay_fwd, ret_decay_bwd, q_norm_g,
                     k_norm_g, lam_q1, lam_k1, lam_q2, lam_k2, subln_g, w_out, norm_ffn_g, w_gate, w_up, w_down)
    return (y_prompt, y_sample)
```

```python
import functools
import math

import jax
import jax.numpy as jnp
from jax import lax
from jax.experimental import pallas as pl
from jax.experimental.pallas import tpu as pltpu

D_MODEL = 1024
DEPTH = 1
RET_HEADS = 4
RET_V_DIM = D_MODEL // RET_HEADS
RET_QK_DIM = RET_V_DIM // 2
ROPE_BASE = 10000.0
DIFF_HEADS = 8
DIFF_V_DIM = D_MODEL // DIFF_HEADS
DIFF_SUB_DIM = DIFF_V_DIM // 2
N_BUCKETS = 32
MAX_DISTANCE = 128
D_FF = ((8 * D_MODEL // 3 + 255) // 256) * 256
EPS = 1e-6

LANES = 128
MXU_DIM = 256
VMEM_LIMIT_BYTES = 56 * 1024 * 1024

RET_CHUNK = 128
KEY_TILE = LANES
Q_TILE = MXU_DIM
BAND_LO, BAND_HI = -2, 3
N_BAND = BAND_HI - BAND_LO + 1
PROJ_TM = 512
OUT_TM = 256
FF_SPLIT = 2

assert MAX_DISTANCE <= KEY_TILE, "bias tiles outside the band must be constant"
assert D_FF % (FF_SPLIT * LANES) == 0

F32 = jnp.float32
BF16 = jnp.bfloat16
NT_DIMS = (((1,), (1,)), ((), ()))
TN_DIMS = (((0,), (0,)), ((), ()))


def _params(semantics):
    return pltpu.CompilerParams(dimension_semantics=semantics, vmem_limit_bytes=VMEM_LIMIT_BYTES)


def _resident(shape):
    return pl.BlockSpec(shape, lambda *_: (0,) * len(shape), pipeline_mode=pl.Buffered(1))


def _sigmoid(y):
    return 1.0 / (1.0 + jnp.exp(-y))


def _rms_cast(x_ref, g_ref):
    x = x_ref[...]
    ms = jnp.mean(x * x, axis=-1, keepdims=True)
    return (x * lax.rsqrt(ms + EPS) * g_ref[...]).astype(BF16)


def _proj_rotary_kernel(x_ref, g_ref, w_ref, cos_ref, sin_ref, q_ref, k_ref):
    y = jnp.dot(_rms_cast(x_ref, g_ref), w_ref[...], preferred_element_type=F32)
    cos = cos_ref[...]
    sin = sin_ref[...]
    half = RET_QK_DIM // 2
    k_off = RET_HEADS * RET_QK_DIM
    for h in range(RET_HEADS):
        q = y[:, h * RET_QK_DIM:(h + 1) * RET_QK_DIM]
        k = y[:, k_off + h * RET_QK_DIM:k_off + (h + 1) * RET_QK_DIM]
        q_ref[h] = (q * cos + pltpu.roll(q, half, 1) * sin).astype(BF16)
        k_ref[h] = ((k * cos + pltpu.roll(k, half, 1) * sin) * (RET_QK_DIM ** -0.5)).astype(BF16)


def _proj_heads_kernel(x_ref, g_ref, w_ref, o_ref, *, heads, width, swish):
    y = jnp.dot(_rms_cast(x_ref, g_ref), w_ref[...], preferred_element_type=F32)
    if swish:
        y = y * _sigmoid(y)
    for h in range(heads):
        o_ref[h] = y[:, h * width:(h + 1) * width].astype(BF16)


def _proj_qknorm_kernel(x_ref, g_ref, w_ref, avg_ref, gn_ref, o_ref, *, scale):
    y = jnp.dot(_rms_cast(x_ref, g_ref), w_ref[...], preferred_element_type=F32)
    yy = (y * y).astype(BF16)
    avg = avg_ref[...]
    ms = jnp.concatenate(
        [jnp.dot(yy[:, j * MXU_DIM:(j + 1) * MXU_DIM], avg, preferred_element_type=F32)
         for j in range(y.shape[1] // MXU_DIM)], axis=1)
    yn = y * lax.rsqrt(ms + EPS) * gn_ref[...]
    if scale != 1.0:
        yn = yn * scale
    for h in range(DIFF_HEADS):
        o_ref[h] = yn[:, h * DIFF_V_DIM:(h + 1) * DIFF_V_DIM].astype(BF16)


def _proj_gate_kernel(x_ref, g_ref, w_ref, o_ref):
    y = jnp.dot(_rms_cast(x_ref, g_ref), w_ref[...], preferred_element_type=F32)
    o_ref[...] = _sigmoid(y).astype(BF16)


def _proj_call(kernel, x2, g, w, extra, extra_specs, out_shape, out_spec, name):
    t, d = x2.shape
    n = w.shape[1]
    tm = PROJ_TM
    return pl.pallas_call(
        kernel,
        grid=(t // tm,),
        in_specs=[pl.BlockSpec((tm, d), lambda i: (i, 0)), _resident((1, d)), _resident((d, n))]
        + extra_specs,
        out_specs=out_spec,
        out_shape=out_shape,
        compiler_params=_params(("parallel",)),
        name=name,
    )(x2, g, w, *extra)


def _head_major(heads, t, width):
    tm = PROJ_TM
    return (jax.ShapeDtypeStruct((heads, t, width), BF16),
            pl.BlockSpec((heads, tm, width), lambda i: (0, i, 0)))


def _bias_kernel(table_ref, o_ref):
    head = pl.program_id(0)
    nb = N_BUCKETS // 2
    max_exact = nb // 2
    j = lax.broadcasted_iota(jnp.int32, (KEY_TILE, Q_TILE), 0)
    i = lax.broadcasted_iota(jnp.int32, (KEY_TILE, Q_TILE), 1)
    for d in range(N_BAND):
        rel = (d + BAND_LO) * KEY_TILE + j - i
        ret = jnp.where(rel > 0, nb, 0)
        n = jnp.abs(rel)
        nf = jnp.maximum(n, 1).astype(F32)
        large = max_exact + (jnp.log(nf / max_exact) / math.log(MAX_DISTANCE / max_exact)
                             * (nb - max_exact)).astype(jnp.int32)
        large = jnp.minimum(large, nb - 1)
        bucket = ret + jnp.where(n < max_exact, n, large)
        acc = jnp.zeros((KEY_TILE, Q_TILE), F32)
        for b in range(N_BUCKETS):
            acc = jnp.where(bucket == b, table_ref[b, head], acc)
        o_ref[0, d] = acc


def _bias_call(table):
    return pl.pallas_call(
        _bias_kernel,
        grid=(DIFF_HEADS,),
        in_specs=[pl.BlockSpec(memory_space=pltpu.SMEM)],
        out_specs=pl.BlockSpec((1, N_BAND, KEY_TILE, Q_TILE), lambda h: (h, 0, 0, 0)),
        out_shape=jax.ShapeDtypeStruct((DIFF_HEADS, N_BAND, KEY_TILE, Q_TILE), F32),
        compiler_params=_params(("arbitrary",)),
        name="rel_bias",
    )(table)


def _retention_kernel(dec_ref, q_ref, k_ref, v_ref, gate_ref, o_ref, acc_ref, st_ref):
    c = RET_CHUNK
    s_len = q_ref.shape[2]
    n_chunks = s_len // c
    dec = dec_ref[0]
    lg_f = jnp.log1p(-jnp.exp(dec[0:1]))
    lg_b = jnp.log1p(-jnp.exp(dec[1:2]))
    row_cc = lax.broadcasted_iota(jnp.int32, (c, c), 0)
    col_cc = lax.broadcasted_iota(jnp.int32, (c, c), 1)
    diff = (row_cc - col_cc).astype(F32)
    decay_mask = jnp.where(diff >= 0, jnp.exp(lg_f[:, :c] * diff), jnp.exp(lg_b[:, :c] * (-diff)))
    row_v = lax.broadcasted_iota(jnp.int32, (c, RET_V_DIM), 0).astype(F32)
    row_k = lax.broadcasted_iota(jnp.int32, (c, RET_QK_DIM), 0).astype(F32)
    xi_f = jnp.exp(lg_f * (row_v + 1.0))
    xi_b = jnp.exp(lg_b * (c - row_v))
    zeta_f = jnp.exp(lg_f[:, :RET_QK_DIM] * (c - 1.0 - row_k))
    zeta_b = jnp.exp(lg_b[:, :RET_QK_DIM] * row_k)
    cd_f = jnp.exp(lg_f * c)
    cd_b = jnp.exp(lg_b * c)

    def chunk(ref, i):
        return ref[0, 0, pl.ds(pl.multiple_of(i * c, c), c), :]

    def state_update(st, k, v, zeta, cd):
        kz = (k.astype(F32) * zeta).astype(BF16)
        return st * cd + lax.dot_general(kz, v, TN_DIMS, preferred_element_type=F32)

    st_ref[...] = jnp.zeros_like(st_ref)

    def fwd(i, carry):
        q, k, v = chunk(q_ref, i), chunk(k_ref, i), chunk(v_ref, i)
        scores = lax.dot_general(q, k, NT_DIMS, preferred_element_type=F32) * decay_mask
        intra = jnp.dot(scores.astype(BF16), v, preferred_element_type=F32)
        st = st_ref[...]
        inter = jnp.dot(q, st.astype(BF16), preferred_element_type=F32) * xi_f
        acc_ref[pl.ds(pl.multiple_of(i * c, c), c), :] = intra + inter
        st_ref[...] = state_update(st, k, v, zeta_f, cd_f)
        return carry

    lax.fori_loop(0, n_chunks, fwd, 0)
    st_ref[...] = jnp.zeros_like(st_ref)

    def bwd(t, carry):
        i = n_chunks - 1 - t
        q, k, v = chunk(q_ref, i), chunk(k_ref, i), chunk(v_ref, i)
        st = st_ref[...]
        inter = jnp.dot(q, st.astype(BF16), preferred_element_type=F32) * xi_b
        rows = pl.ds(pl.multiple_of(i * c, c), c)
        o = acc_ref[rows, :] + inter
        ms = jnp.mean(o * o, axis=-1, keepdims=True)
        gate = gate_ref[0, 0, rows, :].astype(F32)
        o_ref[0, 0, rows, :] = (gate * (o * lax.rsqrt(ms + EPS))).astype(BF16)
        st_ref[...] = state_update(st, k, v, zeta_b, cd_b)
        return carry

    lax.fori_loop(0, n_chunks, bwd, 0)


def _retention_call(dec, q, k, v, gate, batch, s_len):
    def blk(width):
        return pl.BlockSpec((1, 1, s_len, width), lambda b, h: (h, b, 0, 0))

    t = batch * s_len
    r4 = lambda a: a.reshape(RET_HEADS, batch, s_len, a.shape[-1])
    out = pl.pallas_call(
        _retention_kernel,
        grid=(batch, RET_HEADS),
        in_specs=[pl.BlockSpec((1, 2, RET_V_DIM), lambda b, h: (h, 0, 0)),
                  blk(RET_QK_DIM), blk(RET_QK_DIM), blk(RET_V_DIM), blk(RET_V_DIM)],
        out_specs=blk(RET_V_DIM),
        out_shape=jax.ShapeDtypeStruct((RET_HEADS, batch, s_len, RET_V_DIM), BF16),
        scratch_shapes=[pltpu.VMEM((s_len, RET_V_DIM), F32), pltpu.VMEM((RET_QK_DIM, RET_V_DIM), F32)],
        compiler_params=_params(("parallel", "parallel")),
        name="retention",
    )(dec, r4(q), r4(k), r4(v), r4(gate))
    return out.reshape(RET_HEADS, t, RET_V_DIM)


def _diff_attn_kernel(lam_ref, q_ref, k_ref, v_ref, band_ref, g_ref, o_ref, *, lambda_init):
    qi = pl.program_id(2)
    s_len = k_ref.shape[2]
    n_kt = s_len // KEY_TILE
    q = q_ref[0, 0]
    k = k_ref[0, 0]
    v = v_ref[0, 0]
    lane = lax.broadcasted_iota(jnp.int32, q.shape, 1)
    lam_vec = lam_ref[...]
    lam = (jnp.exp(jnp.sum(lam_vec[0:1] * lam_vec[1:2], keepdims=True))
           - jnp.exp(jnp.sum(lam_vec[2:3] * lam_vec[3:4], keepdims=True)) + lambda_init)

    def one_map(c):
        in_map = (lane < DIFF_SUB_DIM) if c == 0 else (lane >= DIFF_SUB_DIM)
        qc = jnp.where(in_map, q, jnp.zeros_like(q))
        s = lax.dot_general(k, qc, NT_DIMS, preferred_element_type=F32)
        tiles = []
        m = None
        for kt in range(n_kt):
            d = jnp.clip(kt - (Q_TILE // KEY_TILE) * qi, BAND_LO, BAND_HI) - BAND_LO
            t = s[kt * KEY_TILE:(kt + 1) * KEY_TILE] + band_ref[0, d]
            tiles.append(t)
            m = t if m is None else jnp.maximum(m, t)
        m = jnp.max(m, axis=0, keepdims=True)
        denom = jnp.zeros((KEY_TILE, Q_TILE), F32)
        e_tiles = []
        for t in tiles:
            e = jnp.exp(t - m)
            denom = denom + e
            e_tiles.append(e.astype(BF16))
        e_all = jnp.concatenate(e_tiles, axis=0)
        denom = jnp.sum(denom, axis=0, keepdims=True)
        o_t = lax.dot_general(v, e_all, TN_DIMS, preferred_element_type=F32)
        return o_t * (1.0 / denom)

    o_t = one_map(0) - lam * one_map(1)
    ms = jnp.mean(o_t * o_t, axis=0, keepdims=True)
    out_t = o_t * lax.rsqrt(ms + EPS) * g_ref[...] * (1.0 - lambda_init)
    o_ref[0, 0] = out_t.T.astype(BF16)


def _diff_attn_call(lam_vec, q, k, v, band, g_bcast, batch, s_len, lambda_init):
    t = batch * s_len
    r4 = lambda a: a.reshape(DIFF_HEADS, batch, s_len, DIFF_V_DIM)
    full = pl.BlockSpec((1, 1, s_len, DIFF_V_DIM), lambda b, h, i: (h, b, 0, 0))
    tile = pl.BlockSpec((1, 1, Q_TILE, DIFF_V_DIM), lambda b, h, i: (h, b, i, 0))
    out = pl.pallas_call(
        functools.partial(_diff_attn_kernel, lambda_init=lambda_init),
        grid=(batch, DIFF_HEADS, s_len // Q_TILE),
        in_specs=[_resident((4, DIFF_SUB_DIM)), tile, full, full,
                  pl.BlockSpec((1, N_BAND, KEY_TILE, Q_TILE), lambda b, h, i: (h, 0, 0, 0)),
                  _resident((DIFF_V_DIM, Q_TILE))],
        out_specs=tile,
        out_shape=jax.ShapeDtypeStruct((DIFF_HEADS, batch, s_len, DIFF_V_DIM), BF16),
        compiler_params=_params(("parallel", "parallel", "parallel")),
        name="diff_attn",
    )(lam_vec, r4(q), r4(k), r4(v), band, g_bcast)
    return out.reshape(DIFF_HEADS, t, DIFF_V_DIM)


def _output_kernel(x_ref, ret_ref, dif_ref, mg_ref, wo_ref, gf_ref, wg_ref, wu_ref, wd_ref, o_ref):
    ret = jnp.concatenate([ret_ref[h] for h in range(RET_HEADS)], axis=1).astype(F32)
    dif = jnp.concatenate([dif_ref[h] for h in range(DIFF_HEADS)], axis=1).astype(F32)
    mg = mg_ref[...].astype(F32)
    merged = (mg[:, :D_MODEL] * ret + mg[:, D_MODEL:] * dif).astype(BF16)
    x1 = x_ref[...] + jnp.dot(merged, wo_ref[...], preferred_element_type=F32)
    ms = jnp.mean(x1 * x1, axis=-1, keepdims=True)
    h = (x1 * lax.rsqrt(ms + EPS) * gf_ref[...]).astype(BF16)
    acc = x1
    ff = D_FF // FF_SPLIT
    for part in range(FF_SPLIT):
        cols = slice(part * ff, (part + 1) * ff)
        gate = jnp.dot(h, wg_ref[:, cols], preferred_element_type=F32)
        up = jnp.dot(h, wu_ref[:, cols], preferred_element_type=F32)
        act = (gate * _sigmoid(gate) * up).astype(BF16)
        acc = acc + jnp.dot(act, wd_ref[cols, :], preferred_element_type=F32)
    o_ref[...] = acc


def _output_call(x2, ret, dif, mg, wo, gf, wg, wu, wd):
    t, d = x2.shape
    tm = OUT_TM
    return pl.pallas_call(
        _output_kernel,
        grid=(t // tm,),
        in_specs=[pl.BlockSpec((tm, d), lambda i: (i, 0)),
                  pl.BlockSpec((RET_HEADS, tm, RET_V_DIM), lambda i: (0, i, 0)),
                  pl.BlockSpec((DIFF_HEADS, tm, DIFF_V_DIM), lambda i: (0, i, 0)),
                  pl.BlockSpec((tm, 2 * d), lambda i: (i, 0)),
                  _resident((d, d)), _resident((1, d)),
                  _resident((d, D_FF)), _resident((d, D_FF)), _resident((D_FF, d))],
        out_specs=pl.BlockSpec((tm, d), lambda i: (i, 0)),
        out_shape=jax.ShapeDtypeStruct((t, d), F32),
        compiler_params=_params(("parallel",)),
        name="merge_ffn",
    )(x2, ret, dif, mg, wo, gf, wg, wu, wd)


def _rotary_tables(s_len):
    half = RET_QK_DIM // 2
    inv = ROPE_BASE ** (-jnp.arange(half, dtype=F32) / half)
    ang = jnp.arange(s_len, dtype=F32)[:, None] * inv[None, :]
    cos, sin = jnp.cos(ang), jnp.sin(ang)
    return jnp.concatenate([cos, cos], axis=-1), jnp.concatenate([-sin, sin], axis=-1)


def _layer_weights(layer, norm_mix_g, w_in, ret_decay_fwd, ret_decay_bwd, q_norm_g, k_norm_g,
                   lam_q1, lam_k1, lam_q2, lam_k2, subln_g, w_out, norm_ffn_g, w_gate, w_up, w_down):
    w = w_in[layer].astype(BF16)
    sizes = (2 * RET_HEADS * RET_QK_DIM, RET_HEADS * RET_V_DIM, RET_HEADS * RET_V_DIM,
             DIFF_HEADS * DIFF_V_DIM, DIFF_HEADS * DIFF_V_DIM, DIFF_HEADS * DIFF_V_DIM, 2 * D_MODEL)
    groups, start = [], 0
    for size in sizes:
        groups.append(w[:, start:start + size])
        start += size
    blk = jnp.arange(MXU_DIM) // DIFF_SUB_DIM
    avg = jnp.where(blk[:, None] == blk[None, :], 1.0 / DIFF_SUB_DIM, 0.0).astype(BF16)
    dec = jnp.stack([ret_decay_fwd[layer], ret_decay_bwd[layer]], axis=1).astype(F32)
    return dict(
        groups=groups,
        norm_mix_g=norm_mix_g[layer].reshape(1, D_MODEL).astype(F32),
        avg=avg,
        q_gain=jnp.tile(q_norm_g[layer].astype(F32), D_MODEL // DIFF_SUB_DIM).reshape(1, D_MODEL),
        k_gain=jnp.tile(k_norm_g[layer].astype(F32), D_MODEL // DIFF_SUB_DIM).reshape(1, D_MODEL),
        dec=jnp.broadcast_to(dec[:, :, None], (RET_HEADS, 2, RET_V_DIM)),
        lam_vec=jnp.stack([lam_q1[layer], lam_k1[layer], lam_q2[layer], lam_k2[layer]]).astype(F32),
        subln=jnp.broadcast_to(subln_g[layer].astype(F32)[:, None], (DIFF_V_DIM, Q_TILE)),
        w_out=w_out[layer].astype(BF16),
        norm_ffn_g=norm_ffn_g[layer].reshape(1, D_MODEL).astype(F32),
        w_gate=w_gate[layer].astype(BF16), w_up=w_up[layer].astype(BF16),
        w_down=w_down[layer].astype(BF16),
    )


def _encoder_layer(x, layer, lw, band, cos, sin):
    batch, s_len, d = x.shape
    t = batch * s_len
    x2 = x.reshape(t, d)
    g = lw["norm_mix_g"]
    w_rot, w_rv, w_rg, w_dq, w_dk, w_dv, w_mg = lw["groups"]
    tm = PROJ_TM
    pos_spec = pl.BlockSpec((tm, RET_QK_DIM), lambda i: (i % (s_len // tm), 0))
    ret_qk = _head_major(RET_HEADS, t, RET_QK_DIM)
    rq, rk = _proj_call(_proj_rotary_kernel, x2, g, w_rot, [cos, sin], [pos_spec, pos_spec],
                        [ret_qk[0], ret_qk[0]], [ret_qk[1], ret_qk[1]], "proj_rotary")
    ret_v = _head_major(RET_HEADS, t, RET_V_DIM)
    plain_ret = functools.partial(_proj_heads_kernel, heads=RET_HEADS, width=RET_V_DIM)
    rv = _proj_call(functools.partial(plain_ret, swish=False), x2, g, w_rv, [], [], *ret_v, "proj_rv")
    rg = _proj_call(functools.partial(plain_ret, swish=True), x2, g, w_rg, [], [], *ret_v, "proj_rg")
    dif = _head_major(DIFF_HEADS, t, DIFF_V_DIM)
    norm_specs = [_resident((MXU_DIM, MXU_DIM)), _resident((1, D_MODEL))]
    dq = _proj_call(functools.partial(_proj_qknorm_kernel, scale=DIFF_SUB_DIM ** -0.5), x2, g, w_dq,
                    [lw["avg"], lw["q_gain"]], norm_specs, *dif, "proj_dq")
    dk = _proj_call(functools.partial(_proj_qknorm_kernel, scale=1.0), x2, g, w_dk,
                    [lw["avg"], lw["k_gain"]], norm_specs, *dif, "proj_dk")
    dv = _proj_call(functools.partial(_proj_heads_kernel, heads=DIFF_HEADS, width=DIFF_V_DIM,
                                      swish=False), x2, g, w_dv, [], [], *dif, "proj_dv")
    mg = _proj_call(_proj_gate_kernel, x2, g, w_mg, [], [],
                    jax.ShapeDtypeStruct((t, 2 * D_MODEL), BF16),
                    pl.BlockSpec((tm, 2 * D_MODEL), lambda i: (i, 0)), "proj_gates")

    ret_out = _retention_call(lw["dec"], rq, rk, rv, rg, batch, s_len)
    lambda_init = 0.8 - 0.6 * math.exp(-0.3 * layer)
    dif_out = _diff_attn_call(lw["lam_vec"], dq, dk, dv, band, lw["subln"], batch, s_len, lambda_init)
    y2 = _output_call(x2, ret_out, dif_out, mg, lw["w_out"], lw["norm_ffn_g"],
                      lw["w_gate"], lw["w_up"], lw["w_down"])
    return y2.reshape(batch, s_len, d)


def kernel(x_prompt, x_sample, rel_bias_table, norm_mix_g, w_in, ret_decay_fwd, ret_decay_bwd,
           q_norm_g, k_norm_g, lam_q1, lam_k1, lam_q2, lam_k2, subln_g, w_out, norm_ffn_g,
           w_gate, w_up, w_down):
    band = _bias_call(rel_bias_table.astype(F32))
    layers = [_layer_weights(l, norm_mix_g, w_in, ret_decay_fwd, ret_decay_bwd, q_norm_g, k_norm_g,
                             lam_q1, lam_k1, lam_q2, lam_k2, subln_g, w_out, norm_ffn_g,
                             w_gate, w_up, w_down) for l in range(DEPTH)]
    outs = []
    for x in (x_prompt, x_sample):
        cos, sin = _rotary_tables(x.shape[1])
        for l in range(DEPTH):
            x = _encoder_layer(x, l, layers[l], band, cos, sin)
        outs.append(x)
    return tuple(outs)
```

```python
import functools
import math

import jax
import jax.numpy as jnp
from jax import lax
from jax.experimental import pallas as pl
from jax.experimental.pallas import tpu as pltpu

D_MODEL = 1024
DEPTH = 1
RET_HEADS = 4
RET_V_DIM = D_MODEL // RET_HEADS
RET_QK_DIM = RET_V_DIM // 2
ROPE_BASE = 10000.0
DIFF_HEADS = 8
DIFF_V_DIM = D_MODEL // DIFF_HEADS
DIFF_SUB_DIM = DIFF_V_DIM // 2
N_BUCKETS = 32
MAX_DISTANCE = 128
D_FF = ((8 * D_MODEL // 3 + 255) // 256) * 256
EPS = 1e-6

LANES = 128
MXU_DIM = 256
VMEM_LIMIT_BYTES = 56 * 1024 * 1024

RET_CHUNK = 256
KEY_TILE = MXU_DIM
Q_TILE = MXU_DIM
KEY_CHUNK = 1024
Q_UNROLL = 4
REACH_SLACK = 1.02
BAND = 2
N_BAND = 2 * BAND + 1
LOG2E = math.log2(math.e)
SHIFT_BOUND_LIMIT = 80.0
PROJ_TM = 512
OUT_TM = 256
FF_SPLIT = 2

assert (BAND - 1) * KEY_TILE + 1 >= MAX_DISTANCE, "bias tiles outside the band must be constant"
assert KEY_TILE == Q_TILE
assert D_FF % (FF_SPLIT * LANES) == 0

F32 = jnp.float32
BF16 = jnp.bfloat16


def _params(semantics):
    return pltpu.CompilerParams(dimension_semantics=semantics, vmem_limit_bytes=VMEM_LIMIT_BYTES)


def _resident(shape):
    return pl.BlockSpec(shape, lambda *_: (0,) * len(shape), pipeline_mode=pl.Buffered(1))


def _sigmoid(y):
    return 1.0 / (1.0 + jnp.exp(-y))


def _rms_cast(x_ref, g_ref):
    x = x_ref[...]
    ms = jnp.mean(x * x, axis=-1, keepdims=True)
    return (x * lax.rsqrt(ms + EPS) * g_ref[...]).astype(BF16)


def _proj_rotary_kernel(x_ref, g_ref, w_ref, cos_ref, sin_ref, q_ref, kt_ref):
    y = jnp.dot(_rms_cast(x_ref, g_ref), w_ref[...], preferred_element_type=F32)
    cos = cos_ref[...]
    sin = sin_ref[...]
    half = RET_QK_DIM // 2
    k_off = RET_HEADS * RET_QK_DIM
    for h in range(RET_HEADS):
        q = y[:, h * RET_QK_DIM:(h + 1) * RET_QK_DIM]
        k = y[:, k_off + h * RET_QK_DIM:k_off + (h + 1) * RET_QK_DIM]
        q_ref[h] = (q * cos + pltpu.roll(q, half, 1) * sin).astype(BF16)
        kt_ref[h] = ((k * cos + pltpu.roll(k, half, 1) * sin) * (RET_QK_DIM ** -0.5)).T.astype(BF16)


def _proj_heads_kernel(x_ref, g_ref, w_ref, o_ref, *, heads, width, swish):
    y = jnp.dot(_rms_cast(x_ref, g_ref), w_ref[...], preferred_element_type=F32)
    if swish:
        y = y * _sigmoid(y)
    for h in range(heads):
        o_ref[h] = y[:, h * width:(h + 1) * width].astype(BF16)


def _proj_heads_t_kernel(x_ref, g_ref, w_ref, o_ref, *, heads, width):
    y = jnp.dot(_rms_cast(x_ref, g_ref), w_ref[...], preferred_element_type=F32)
    for h in range(heads):
        o_ref[h] = y[:, h * width:(h + 1) * width].T.astype(BF16)


def _proj_qknorm_kernel(x_ref, g_ref, w_ref, avg_ref, gn_ref, o_ref, *, scale, transposed):
    y = jnp.dot(_rms_cast(x_ref, g_ref), w_ref[...], preferred_element_type=F32)
    yy = (y * y).astype(BF16)
    avg = avg_ref[...]
    ms = jnp.concatenate(
        [jnp.dot(yy[:, j * MXU_DIM:(j + 1) * MXU_DIM], avg, preferred_element_type=F32)
         for j in range(y.shape[1] // MXU_DIM)], axis=1)
    yn = y * lax.rsqrt(ms + EPS) * gn_ref[...]
    if scale != 1.0:
        yn = yn * scale
    for h in range(DIFF_HEADS):
        head = yn[:, h * DIFF_V_DIM:(h + 1) * DIFF_V_DIM]
        o_ref[h] = (head.T if transposed else head).astype(BF16)


def _proj_gate_kernel(x_ref, g_ref, w_ref, o_ref):
    y = jnp.dot(_rms_cast(x_ref, g_ref), w_ref[...], preferred_element_type=F32)
    o_ref[...] = _sigmoid(y).astype(BF16)


def _proj_call(kernel, x2, g, w, extra, extra_specs, out_shape, out_spec, name):
    t, d = x2.shape
    n = w.shape[1]
    tm = PROJ_TM
    return pl.pallas_call(
        kernel,
        grid=(t // tm,),
        in_specs=[pl.BlockSpec((tm, d), lambda i: (i, 0)), _resident((1, d)), _resident((d, n))]
        + extra_specs,
        out_specs=out_spec,
        out_shape=out_shape,
        compiler_params=_params(("parallel",)),
        name=name,
    )(x2, g, w, *extra)


def _head_major(heads, t, width):
    tm = PROJ_TM
    return (jax.ShapeDtypeStruct((heads, t, width), BF16),
            pl.BlockSpec((heads, tm, width), lambda i: (0, i, 0)))


def _bias_kernel(table_ref, gq_ref, gk_ref, o_ref, stat_ref):
    head = pl.program_id(0)
    nb = N_BUCKETS // 2
    max_exact = nb // 2
    j = lax.broadcasted_iota(jnp.int32, (KEY_TILE, Q_TILE), 0)
    i = lax.broadcasted_iota(jnp.int32, (KEY_TILE, Q_TILE), 1)
    tiles = []
    hi = lo = None
    for d in range(N_BAND):
        rel = (d - BAND) * KEY_TILE + j - i
        ret = jnp.where(rel > 0, nb, 0)
        n = jnp.abs(rel)
        nf = jnp.maximum(n, 1).astype(F32)
        large = max_exact + (jnp.log(nf / max_exact) / math.log(MAX_DISTANCE / max_exact)
                             * (nb - max_exact)).astype(jnp.int32)
        large = jnp.minimum(large, nb - 1)
        bucket = ret + jnp.where(n < max_exact, n, large)
        acc = jnp.zeros((KEY_TILE, Q_TILE), F32)
        for b in range(N_BUCKETS):
            acc = jnp.where(bucket == b, table_ref[b, head], acc)
        acc = acc * LOG2E
        tiles.append(acc)
        hi = acc if hi is None else jnp.maximum(hi, acc)
        lo = acc if lo is None else jnp.minimum(lo, acc)
    hi = jnp.max(jnp.max(hi, axis=0, keepdims=True), axis=1, keepdims=True)
    lo = jnp.min(jnp.min(lo, axis=0, keepdims=True), axis=1, keepdims=True)
    reach = (REACH_SLACK * LOG2E * DIFF_SUB_DIM ** 0.5
             * jnp.max(jnp.abs(gq_ref[...]), axis=1, keepdims=True)
             * jnp.max(jnp.abs(gk_ref[...]), axis=1, keepdims=True))
    shift = reach + hi
    for d in range(N_BAND):
        o_ref[0, d] = tiles[d] - shift
    spread = 2.0 * reach + hi - lo
    stat_ref[0] = jnp.broadcast_to(jnp.where(spread > SHIFT_BOUND_LIMIT, 1.0, 0.0), (8, LANES))


def _bias_call(table, gq, gk):
    return pl.pallas_call(
        _bias_kernel,
        grid=(DIFF_HEADS,),
        in_specs=[pl.BlockSpec(memory_space=pltpu.SMEM), _resident((1, DIFF_SUB_DIM)),
                  _resident((1, DIFF_SUB_DIM))],
        out_specs=[pl.BlockSpec((1, N_BAND, KEY_TILE, Q_TILE), lambda h: (h, 0, 0, 0)),
                   pl.BlockSpec((1, 8, LANES), lambda h: (h, 0, 0))],
        out_shape=[jax.ShapeDtypeStruct((DIFF_HEADS, N_BAND, KEY_TILE, Q_TILE), F32),
                   jax.ShapeDtypeStruct((DIFF_HEADS, 8, LANES), F32)],
        compiler_params=_params(("arbitrary",)),
        name="rel_bias",
    )(table, gq, gk)


def _retention_kernel(dec_ref, q_ref, kt_ref, v_ref, gate_ref, o_ref):
    c = RET_CHUNK
    s_len = q_ref.shape[2]
    n_chunks = s_len // c
    dec = dec_ref[0]
    lg_f = jnp.log1p(-jnp.exp(dec[0:1]))
    lg_b = jnp.log1p(-jnp.exp(dec[1:2]))
    row_cc = lax.broadcasted_iota(jnp.int32, (c, c), 0)
    col_cc = lax.broadcasted_iota(jnp.int32, (c, c), 1)
    diff = (row_cc - col_cc).astype(F32)
    decay_mask = jnp.where(diff >= 0, jnp.exp(lg_f[:, :c] * diff), jnp.exp(lg_b[:, :c] * (-diff)))
    pos = lax.broadcasted_iota(jnp.int32, (1, c), 1).astype(F32)
    zeta_f = jnp.exp(lg_f[:, :c] * (c - 1.0 - pos))
    zeta_b = jnp.exp(lg_b[:, :c] * pos)
    row_v = lax.broadcasted_iota(jnp.int32, (c, RET_V_DIM), 0).astype(F32)
    xi_f = jnp.exp(lg_f * (row_v + 1.0))
    xi_b = jnp.exp(lg_b * (c - row_v))
    cd_f = jnp.exp(lg_f * c)
    cd_b = jnp.exp(lg_b * c)

    def rows(i):
        return slice(i * c, (i + 1) * c)

    kv = []
    for i in range(n_chunks):
        kt = kt_ref[0, :, rows(i)].astype(F32)
        kz = jnp.concatenate([kt * zeta_f, kt * zeta_b], axis=0).astype(BF16)
        kv.append(jnp.dot(kz, v_ref[0, 0, rows(i), :], preferred_element_type=F32))
    state = jnp.zeros((RET_QK_DIM, RET_V_DIM), F32)
    state_f = []
    for i in range(n_chunks):
        state_f.append(state.astype(BF16))
        state = state * cd_f + kv[i][:RET_QK_DIM]
    state = jnp.zeros((RET_QK_DIM, RET_V_DIM), F32)
    state_b = [None] * n_chunks
    for i in reversed(range(n_chunks)):
        state_b[i] = state.astype(BF16)
        state = state * cd_b + kv[i][RET_QK_DIM:]
    for i in range(n_chunks):
        q = q_ref[0, 0, rows(i), :]
        scores = jnp.dot(q, kt_ref[0, :, rows(i)], preferred_element_type=F32) * decay_mask
        o = (jnp.dot(scores.astype(BF16), v_ref[0, 0, rows(i), :], preferred_element_type=F32)
             + jnp.dot(q, state_f[i], preferred_element_type=F32) * xi_f
             + jnp.dot(q, state_b[i], preferred_element_type=F32) * xi_b)
        ms = jnp.mean(o * o, axis=-1, keepdims=True)
        gate = gate_ref[0, 0, rows(i), :].astype(F32)
        o_ref[0, 0, rows(i), :] = (gate * (o * lax.rsqrt(ms + EPS))).astype(BF16)


def _retention_call(dec, q, kt, v, gate, batch, s_len):
    def blk(width):
        return pl.BlockSpec((1, 1, s_len, width), lambda b, h: (h, b, 0, 0))

    t = batch * s_len
    r4 = lambda a: a.reshape(RET_HEADS, batch, s_len, a.shape[-1])
    out = pl.pallas_call(
        _retention_kernel,
        grid=(batch, RET_HEADS),
        in_specs=[pl.BlockSpec((1, 2, RET_V_DIM), lambda b, h: (h, 0, 0)),
                  blk(RET_QK_DIM), pl.BlockSpec((1, RET_QK_DIM, s_len), lambda b, h: (h, 0, b)),
                  blk(RET_V_DIM), blk(RET_V_DIM)],
        out_specs=blk(RET_V_DIM),
        out_shape=jax.ShapeDtypeStruct((RET_HEADS, batch, s_len, RET_V_DIM), BF16),
        compiler_params=_params(("parallel", "parallel")),
        name="retention",
    )(dec, r4(q), kt, r4(v), r4(gate))
    return out.reshape(RET_HEADS, t, RET_V_DIM)


def _diff_attn_kernel(flag_ref, lam_ref, qt_ref, k_ref, vt_ref, band_ref, g_ref, o_ref, *, lambda_init):
    head = pl.program_id(1)
    step = pl.program_id(2)
    s_len = k_ref.shape[2]
    n_kt = s_len // KEY_TILE
    sub = KEY_CHUNK // KEY_TILE
    lam_vec = lam_ref[...]
    lam = (jnp.exp(jnp.sum(lam_vec[0:1] * lam_vec[1:2], keepdims=True))
           - jnp.exp(jnp.sum(lam_vec[2:3] * lam_vec[3:4], keepdims=True)) + lambda_init)

    def query_maps(u):
        qt = qt_ref[0, :, u * Q_TILE:(u + 1) * Q_TILE]
        in_map0 = lax.broadcasted_iota(jnp.int32, qt.shape, 0) < DIFF_SUB_DIM
        zeros = jnp.zeros_like(qt)
        return jnp.concatenate([jnp.where(in_map0, qt, zeros), jnp.where(in_map0, zeros, qt)], axis=1)

    def bias_tile(t, qi):
        b = band_ref[0, jnp.clip(t - qi, -BAND, BAND) + BAND]
        return jnp.concatenate([b, b], axis=1)

    def finish(u, acc, denom):
        o2 = acc * (1.0 / denom)
        o_t = o2[:, :Q_TILE] - lam * o2[:, Q_TILE:]
        ms = jnp.mean(o_t * o_t, axis=0, keepdims=True)
        out_t = o_t * lax.rsqrt(ms + EPS) * g_ref[...] * (1.0 - lambda_init)
        o_ref[0, 0, u * Q_TILE:(u + 1) * Q_TILE, :] = out_t.T.astype(BF16)

    @pl.when(flag_ref[head] == 0)
    def _bounded():
        for u in range(Q_UNROLL):
            qi = step * Q_UNROLL + u
            q_maps = query_maps(u)
            acc = jnp.zeros((DIFF_V_DIM, 2 * Q_TILE), F32)
            denom = jnp.zeros((8, 2 * Q_TILE), F32)
            for c in range(s_len // KEY_CHUNK):
                rows = slice(c * KEY_CHUNK, (c + 1) * KEY_CHUNK)
                s = jnp.dot(k_ref[0, 0, rows, :], q_maps, preferred_element_type=F32)
                bias = jnp.concatenate([bias_tile(c * sub + j, qi) for j in range(sub)], axis=0)
                e = jnp.exp2(s + bias)
                denom = denom + jnp.sum(e.reshape(KEY_CHUNK // 8, 8, 2 * Q_TILE), axis=0)
                acc = acc + jnp.dot(vt_ref[0, :, rows], e.astype(BF16), preferred_element_type=F32)
            finish(u, acc, jnp.sum(denom, axis=0, keepdims=True))

    @pl.when(flag_ref[head] != 0)
    def _exact_maxima():
        for u in range(Q_UNROLL):
            qi = step * Q_UNROLL + u
            q_maps = query_maps(u)

            def logits(t):
                rows = pl.ds(pl.multiple_of(t * KEY_TILE, KEY_TILE), KEY_TILE)
                return jnp.dot(k_ref[0, 0, rows, :], q_maps, preferred_element_type=F32) + bias_tile(t, qi)

            def max_body(t, m):
                return jnp.maximum(m, jnp.max(logits(t), axis=0, keepdims=True))

            m = lax.fori_loop(0, n_kt, max_body, jnp.full((1, 2 * Q_TILE), jnp.finfo(F32).min, F32))

            def sum_body(t, carry):
                acc, denom = carry
                e = jnp.exp2(logits(t) - m)
                cols = pl.ds(pl.multiple_of(t * KEY_TILE, KEY_TILE), KEY_TILE)
                acc = acc + jnp.dot(vt_ref[0, :, cols], e.astype(BF16), preferred_element_type=F32)
                return acc, denom + jnp.sum(e, axis=0, keepdims=True)

            acc, denom = lax.fori_loop(
                0, n_kt, sum_body,
                (jnp.zeros((DIFF_V_DIM, 2 * Q_TILE), F32), jnp.zeros((1, 2 * Q_TILE), F32)))
            finish(u, acc, denom)


def _diff_attn_call(flags, lam_vec, qt, k, vt, band, g_bcast, batch, s_len, lambda_init):
    t = batch * s_len
    rows = Q_UNROLL * Q_TILE
    n_steps = s_len // rows
    out = pl.pallas_call(
        functools.partial(_diff_attn_kernel, lambda_init=lambda_init),
        grid=(batch, DIFF_HEADS, n_steps),
        in_specs=[pl.BlockSpec(memory_space=pltpu.SMEM), _resident((4, DIFF_SUB_DIM)),
                  pl.BlockSpec((1, DIFF_V_DIM, rows), lambda b, h, i: (h, 0, b * n_steps + i)),
                  pl.BlockSpec((1, 1, s_len, DIFF_V_DIM), lambda b, h, i: (h, b, 0, 0)),
                  pl.BlockSpec((1, DIFF_V_DIM, s_len), lambda b, h, i: (h, 0, b)),
                  pl.BlockSpec((1, N_BAND, KEY_TILE, Q_TILE), lambda b, h, i: (h, 0, 0, 0)),
                  _resident((DIFF_V_DIM, Q_TILE))],
        out_specs=pl.BlockSpec((1, 1, rows, DIFF_V_DIM), lambda b, h, i: (h, b, i, 0)),
        out_shape=jax.ShapeDtypeStruct((DIFF_HEADS, batch, s_len, DIFF_V_DIM), BF16),
        compiler_params=_params(("parallel", "parallel", "parallel")),
        name="diff_attn",
    )(flags, lam_vec, qt, k.reshape(DIFF_HEADS, batch, s_len, DIFF_V_DIM), vt, band, g_bcast)
    return out.reshape(DIFF_HEADS, t, DIFF_V_DIM)


def _output_kernel(x_ref, ret_ref, dif_ref, mg_ref, wo_ref, gf_ref, wg_ref, wu_ref, wd_ref, o_ref):
    ret = jnp.concatenate([ret_ref[h] for h in range(RET_HEADS)], axis=1).astype(F32)
    dif = jnp.concatenate([dif_ref[h] for h in range(DIFF_HEADS)], axis=1).astype(F32)
    mg = mg_ref[...].astype(F32)
    merged = (mg[:, :D_MODEL] * ret + mg[:, D_MODEL:] * dif).astype(BF16)
    x1 = x_ref[...] + jnp.dot(merged, wo_ref[...], preferred_element_type=F32)
    ms = jnp.mean(x1 * x1, axis=-1, keepdims=True)
    h = (x1 * lax.rsqrt(ms + EPS) * gf_ref[...]).astype(BF16)
    acc = x1
    ff = D_FF // FF_SPLIT
    for part in range(FF_SPLIT):
        cols = slice(part * ff, (part + 1) * ff)
        gate = jnp.dot(h, wg_ref[:, cols], preferred_element_type=F32)
        up = jnp.dot(h, wu_ref[:, cols], preferred_element_type=F32)
        act = (gate * _sigmoid(gate) * up).astype(BF16)
        acc = acc + jnp.dot(act, wd_ref[cols, :], preferred_element_type=F32)
    o_ref[...] = acc


def _output_call(x2, ret, dif, mg, wo, gf, wg, wu, wd):
    t, d = x2.shape
    tm = OUT_TM
    return pl.pallas_call(
        _output_kernel,
        grid=(t // tm,),
        in_specs=[pl.BlockSpec((tm, d), lambda i: (i, 0)),
                  pl.BlockSpec((RET_HEADS, tm, RET_V_DIM), lambda i: (0, i, 0)),
                  pl.BlockSpec((DIFF_HEADS, tm, DIFF_V_DIM), lambda i: (0, i, 0)),
                  pl.BlockSpec((tm, 2 * d), lambda i: (i, 0)),
                  _resident((d, d)), _resident((1, d)),
                  _resident((d, D_FF)), _resident((d, D_FF)), _resident((D_FF, d))],
        out_specs=pl.BlockSpec((tm, d), lambda i: (i, 0)),
        out_shape=jax.ShapeDtypeStruct((t, d), F32),
        compiler_params=_params(("parallel",)),
        name="merge_ffn",
    )(x2, ret, dif, mg, wo, gf, wg, wu, wd)


def _rotary_tables(s_len):
    half = RET_QK_DIM // 2
    inv = ROPE_BASE ** (-jnp.arange(half, dtype=F32) / half)
    ang = jnp.arange(s_len, dtype=F32)[:, None] * inv[None, :]
    cos, sin = jnp.cos(ang), jnp.sin(ang)
    return jnp.concatenate([cos, cos], axis=-1), jnp.concatenate([-sin, sin], axis=-1)


def _layer_weights(layer, norm_mix_g, w_in, ret_decay_fwd, ret_decay_bwd, q_norm_g, k_norm_g,
                   lam_q1, lam_k1, lam_q2, lam_k2, subln_g, w_out, norm_ffn_g, w_gate, w_up, w_down):
    w = w_in[layer].astype(BF16)
    sizes = (2 * RET_HEADS * RET_QK_DIM, RET_HEADS * RET_V_DIM, RET_HEADS * RET_V_DIM,
             DIFF_HEADS * DIFF_V_DIM, DIFF_HEADS * DIFF_V_DIM, DIFF_HEADS * DIFF_V_DIM, 2 * D_MODEL)
    groups, start = [], 0
    for size in sizes:
        groups.append(w[:, start:start + size])
        start += size
    blk = jnp.arange(MXU_DIM) // DIFF_SUB_DIM
    avg = jnp.where(blk[:, None] == blk[None, :], 1.0 / DIFF_SUB_DIM, 0.0).astype(BF16)
    dec = jnp.stack([ret_decay_fwd[layer], ret_decay_bwd[layer]], axis=1).astype(F32)
    return dict(
        groups=groups,
        norm_mix_g=norm_mix_g[layer].reshape(1, D_MODEL).astype(F32),
        avg=avg,
        q_gain=jnp.tile(q_norm_g[layer].astype(F32), D_MODEL // DIFF_SUB_DIM).reshape(1, D_MODEL),
        k_gain=jnp.tile(k_norm_g[layer].astype(F32), D_MODEL // DIFF_SUB_DIM).reshape(1, D_MODEL),
        dec=jnp.broadcast_to(dec[:, :, None], (RET_HEADS, 2, RET_V_DIM)),
        q_norm_g=q_norm_g[layer].astype(F32).reshape(1, DIFF_SUB_DIM),
        k_norm_g=k_norm_g[layer].astype(F32).reshape(1, DIFF_SUB_DIM),
        lam_vec=jnp.stack([lam_q1[layer], lam_k1[layer], lam_q2[layer], lam_k2[layer]]).astype(F32),
        subln=jnp.broadcast_to(subln_g[layer].astype(F32)[:, None], (DIFF_V_DIM, Q_TILE)),
        w_out=w_out[layer].astype(BF16),
        norm_ffn_g=norm_ffn_g[layer].reshape(1, D_MODEL).astype(F32),
        w_gate=w_gate[layer].astype(BF16), w_up=w_up[layer].astype(BF16),
        w_down=w_down[layer].astype(BF16),
    )


def _encoder_layer(x, layer, lw, bias, cos, sin):
    batch, s_len, d = x.shape
    t = batch * s_len
    x2 = x.reshape(t, d)
    g = lw["norm_mix_g"]
    w_rot, w_rv, w_rg, w_dq, w_dk, w_dv, w_mg = lw["groups"]
    tm = PROJ_TM
    pos_spec = pl.BlockSpec((tm, RET_QK_DIM), lambda i: (i % (s_len // tm), 0))
    ret_qk = _head_major(RET_HEADS, t, RET_QK_DIM)
    ret_kt = (jax.ShapeDtypeStruct((RET_HEADS, RET_QK_DIM, t), BF16),
              pl.BlockSpec((RET_HEADS, RET_QK_DIM, tm), lambda i: (0, 0, i)))
    rq, rkt = _proj_call(_proj_rotary_kernel, x2, g, w_rot, [cos, sin], [pos_spec, pos_spec],
                         [ret_qk[0], ret_kt[0]], [ret_qk[1], ret_kt[1]], "proj_rotary")
    ret_v = _head_major(RET_HEADS, t, RET_V_DIM)
    plain_ret = functools.partial(_proj_heads_kernel, heads=RET_HEADS, width=RET_V_DIM)
    rv = _proj_call(functools.partial(plain_ret, swish=False), x2, g, w_rv, [], [], *ret_v, "proj_rv")
    rg = _proj_call(functools.partial(plain_ret, swish=True), x2, g, w_rg, [], [], *ret_v, "proj_rg")
    dif = _head_major(DIFF_HEADS, t, DIFF_V_DIM)
    norm_specs = [_resident((MXU_DIM, MXU_DIM)), _resident((1, D_MODEL))]
    dif_t = (jax.ShapeDtypeStruct((DIFF_HEADS, DIFF_V_DIM, t), BF16),
             pl.BlockSpec((DIFF_HEADS, DIFF_V_DIM, tm), lambda i: (0, 0, i)))
    dqt = _proj_call(functools.partial(_proj_qknorm_kernel, scale=DIFF_SUB_DIM ** -0.5 * LOG2E,
                                       transposed=True),
                     x2, g, w_dq, [lw["avg"], lw["q_gain"]], norm_specs, *dif_t, "proj_dq")
    dk = _proj_call(functools.partial(_proj_qknorm_kernel, scale=1.0, transposed=False), x2, g, w_dk,
                    [lw["avg"], lw["k_gain"]], norm_specs, *dif, "proj_dk")
    dvt = _proj_call(functools.partial(_proj_heads_t_kernel, heads=DIFF_HEADS, width=DIFF_V_DIM),
                     x2, g, w_dv, [], [], *dif_t, "proj_dv")
    mg = _proj_call(_proj_gate_kernel, x2, g, w_mg, [], [],
                    jax.ShapeDtypeStruct((t, 2 * D_MODEL), BF16),
                    pl.BlockSpec((tm, 2 * D_MODEL), lambda i: (i, 0)), "proj_gates")

    ret_out = _retention_call(lw["dec"], rq, rkt, rv, rg, batch, s_len)
    lambda_init = 0.8 - 0.6 * math.exp(-0.3 * layer)
    band, flags = bias
    dif_out = _diff_attn_call(flags, lw["lam_vec"], dqt, dk, dvt, band, lw["subln"], batch, s_len,
                              lambda_init)
    y2 = _output_call(x2, ret_out, dif_out, mg, lw["w_out"], lw["norm_ffn_g"],
                      lw["w_gate"], lw["w_up"], lw["w_down"])
    return y2.reshape(batch, s_len, d)


def kernel(x_prompt, x_sample, rel_bias_table, norm_mix_g, w_in, ret_decay_fwd, ret_decay_bwd,
           q_norm_g, k_norm_g, lam_q1, lam_k1, lam_q2, lam_k2, subln_g, w_out, norm_ffn_g,
           w_gate, w_up, w_down):
    table = rel_bias_table.astype(F32)
    layers = [_layer_weights(l, norm_mix_g, w_in, ret_decay_fwd, ret_decay_bwd, q_norm_g, k_norm_g,
                             lam_q1, lam_k1, lam_q2, lam_k2, subln_g, w_out, norm_ffn_g,
                             w_gate, w_up, w_down) for l in range(DEPTH)]
    biases = []
    for lw in layers:
        band, stat = _bias_call(table, lw["q_norm_g"], lw["k_norm_g"])
        biases.append((band, (stat[:, 0, 0] > 0.0).astype(jnp.int32)))
    outs = []
    for x in (x_prompt, x_sample):
        cos, sin = _rotary_tables(x.shape[1])
        for l in range(DEPTH):
            x = _encoder_layer(x, l, layers[l], biases[l], cos, sin)
        outs.append(x)
    return tuple(outs)
```

```python
import functools
import math

import jax
import jax.numpy as jnp
from jax import lax
from jax.experimental import pallas as pl
from jax.experimental.pallas import tpu as pltpu

D_MODEL = 1024
DEPTH = 1
RET_HEADS = 4
RET_V_DIM = D_MODEL // RET_HEADS
RET_QK_DIM = RET_V_DIM // 2
ROPE_BASE = 10000.0
DIFF_HEADS = 8
DIFF_V_DIM = D_MODEL // DIFF_HEADS
DIFF_SUB_DIM = DIFF_V_DIM // 2
N_BUCKETS = 32
MAX_DISTANCE = 128
D_FF = ((8 * D_MODEL // 3 + 255) // 256) * 256
EPS = 1e-6

LANES = 128
MXU_DIM = 256
VMEM_LIMIT_BYTES = 56 * 1024 * 1024

RET_CHUNK = 256
KEY_TILE = MXU_DIM
Q_TILE = MXU_DIM
KEY_CHUNK = 1024
Q_UNROLL = 8
REACH_SLACK = 1.02
BAND = 2
N_BAND = 2 * BAND + 1
LOG2E = math.log2(math.e)
SHIFT_BOUND_LIMIT = 80.0
PROJ_TM = 512
OUT_TM = 512
FF_SPLIT = 11

assert (BAND - 1) * KEY_TILE + 1 >= MAX_DISTANCE, "bias tiles outside the band must be constant"
assert KEY_TILE == Q_TILE
assert D_FF % (FF_SPLIT * LANES) == 0

F32 = jnp.float32
BF16 = jnp.bfloat16


def _params(semantics):
    return pltpu.CompilerParams(dimension_semantics=semantics, vmem_limit_bytes=VMEM_LIMIT_BYTES)


def _resident(shape):
    return pl.BlockSpec(shape, lambda *_: (0,) * len(shape), pipeline_mode=pl.Buffered(1))


def _sigmoid(y):
    return 1.0 / (1.0 + jnp.exp(-y))


def _rms_cast(x_ref, g_ref):
    x = x_ref[...]
    ms = jnp.mean(x * x, axis=-1, keepdims=True)
    return (x * lax.rsqrt(ms + EPS) * g_ref[...]).astype(BF16)


_COL_SIZES = (RET_HEADS * RET_QK_DIM, RET_HEADS * RET_QK_DIM, RET_HEADS * RET_V_DIM,
              RET_HEADS * RET_V_DIM, DIFF_HEADS * DIFF_V_DIM, DIFF_HEADS * DIFF_V_DIM,
              DIFF_HEADS * DIFF_V_DIM, 2 * D_MODEL)
COL_RQ, COL_RK, COL_RV, COL_RG, COL_DQ, COL_DK, COL_DV, COL_MG = (
    sum(_COL_SIZES[:i]) for i in range(len(_COL_SIZES)))
IN_COLS = sum(_COL_SIZES)


def _in_proj_kernel(x_ref, g_ref, w_ref, cos_ref, sin_ref, avg_ref, gq_ref, gk_ref,
                    rq_ref, rkt_ref, rv_ref, rg_ref, dqt_ref, dk_ref, dvt_ref, mg_ref):
    h = _rms_cast(x_ref, g_ref)

    def project(start, size):
        return jnp.dot(h, w_ref[:, start:start + size], preferred_element_type=F32)

    def group_rms(y, gain, scale):
        yy = (y * y).astype(BF16)
        avg = avg_ref[...]
        ms = jnp.concatenate(
            [jnp.dot(yy[:, j * MXU_DIM:(j + 1) * MXU_DIM], avg, preferred_element_type=F32)
             for j in range(y.shape[1] // MXU_DIM)], axis=1)
        yn = y * lax.rsqrt(ms + EPS) * gain
        return yn if scale == 1.0 else yn * scale

    cos = cos_ref[...]
    sin = sin_ref[...]
    half = RET_QK_DIM // 2
    y = project(COL_RQ, 2 * RET_HEADS * RET_QK_DIM)
    for hd in range(RET_HEADS):
        q = y[:, hd * RET_QK_DIM:(hd + 1) * RET_QK_DIM]
        k = y[:, COL_RK + hd * RET_QK_DIM:COL_RK + (hd + 1) * RET_QK_DIM]
        rq_ref[hd] = (q * cos + pltpu.roll(q, half, 1) * sin).astype(BF16)
        rkt_ref[hd] = ((k * cos + pltpu.roll(k, half, 1) * sin) * (RET_QK_DIM ** -0.5)).T.astype(BF16)
    y = project(COL_RV, RET_HEADS * RET_V_DIM)
    for hd in range(RET_HEADS):
        rv_ref[hd] = y[:, hd * RET_V_DIM:(hd + 1) * RET_V_DIM].astype(BF16)
    y = project(COL_RG, RET_HEADS * RET_V_DIM)
    y = y * _sigmoid(y)
    for hd in range(RET_HEADS):
        rg_ref[hd] = y[:, hd * RET_V_DIM:(hd + 1) * RET_V_DIM].astype(BF16)
    y = group_rms(project(COL_DQ, DIFF_HEADS * DIFF_V_DIM), gq_ref[...], DIFF_SUB_DIM ** -0.5 * LOG2E)
    for hd in range(DIFF_HEADS):
        dqt_ref[hd] = y[:, hd * DIFF_V_DIM:(hd + 1) * DIFF_V_DIM].T.astype(BF16)
    y = group_rms(project(COL_DK, DIFF_HEADS * DIFF_V_DIM), gk_ref[...], 1.0)
    for hd in range(DIFF_HEADS):
        dk_ref[hd] = y[:, hd * DIFF_V_DIM:(hd + 1) * DIFF_V_DIM].astype(BF16)
    y = project(COL_DV, DIFF_HEADS * DIFF_V_DIM)
    for hd in range(DIFF_HEADS):
        dvt_ref[hd] = y[:, hd * DIFF_V_DIM:(hd + 1) * DIFF_V_DIM].T.astype(BF16)
    for part in range(2):
        y = project(COL_MG + part * D_MODEL, D_MODEL)
        mg_ref[:, part * D_MODEL:(part + 1) * D_MODEL] = _sigmoid(y).astype(BF16)


def _in_proj_call(x2, g, w, cos, sin, avg, gq, gk, s_len):
    t, d = x2.shape
    tm = PROJ_TM

    def rows_major(heads, width):
        return (jax.ShapeDtypeStruct((heads, t, width), BF16),
                pl.BlockSpec((heads, tm, width), lambda i: (0, i, 0)))

    def cols_major(heads, width):
        return (jax.ShapeDtypeStruct((heads, width, t), BF16),
                pl.BlockSpec((heads, width, tm), lambda i: (0, 0, i)))

    outs = [rows_major(RET_HEADS, RET_QK_DIM), cols_major(RET_HEADS, RET_QK_DIM),
            rows_major(RET_HEADS, RET_V_DIM), rows_major(RET_HEADS, RET_V_DIM),
            cols_major(DIFF_HEADS, DIFF_V_DIM), rows_major(DIFF_HEADS, DIFF_V_DIM),
            cols_major(DIFF_HEADS, DIFF_V_DIM),
            (jax.ShapeDtypeStruct((t, 2 * d), BF16), pl.BlockSpec((tm, 2 * d), lambda i: (i, 0)))]
    pos_spec = pl.BlockSpec((tm, RET_QK_DIM), lambda i: (i % (s_len // tm), 0))
    return pl.pallas_call(
        _in_proj_kernel,
        grid=(t // tm,),
        in_specs=[pl.BlockSpec((tm, d), lambda i: (i, 0)), _resident((1, d)), _resident((d, IN_COLS)),
                  pos_spec, pos_spec, _resident((MXU_DIM, MXU_DIM)), _resident((1, d)), _resident((1, d))],
        out_specs=[o[1] for o in outs],
        out_shape=[o[0] for o in outs],
        compiler_params=_params(("parallel",)),
        name="in_proj",
    )(x2, g, w, cos, sin, avg, gq, gk)


def _bias_kernel(table_ref, gq_ref, gk_ref, o_ref, stat_ref):
    head = pl.program_id(0)
    nb = N_BUCKETS // 2
    max_exact = nb // 2
    j = lax.broadcasted_iota(jnp.int32, (KEY_TILE, Q_TILE), 0)
    i = lax.broadcasted_iota(jnp.int32, (KEY_TILE, Q_TILE), 1)
    tiles = []
    hi = lo = None
    for d in range(N_BAND):
        rel = (d - BAND) * KEY_TILE + j - i
        ret = jnp.where(rel > 0, nb, 0)
        n = jnp.abs(rel)
        nf = jnp.maximum(n, 1).astype(F32)
        large = max_exact + (jnp.log(nf / max_exact) / math.log(MAX_DISTANCE / max_exact)
                             * (nb - max_exact)).astype(jnp.int32)
        large = jnp.minimum(large, nb - 1)
        bucket = ret + jnp.where(n < max_exact, n, large)
        acc = jnp.zeros((KEY_TILE, Q_TILE), F32)
        for b in range(N_BUCKETS):
            acc = jnp.where(bucket == b, table_ref[b, head], acc)
        acc = acc * LOG2E
        tiles.append(acc)
        hi = acc if hi is None else jnp.maximum(hi, acc)
        lo = acc if lo is None else jnp.minimum(lo, acc)
    hi = jnp.max(jnp.max(hi, axis=0, keepdims=True), axis=1, keepdims=True)
    lo = jnp.min(jnp.min(lo, axis=0, keepdims=True), axis=1, keepdims=True)
    reach = (REACH_SLACK * LOG2E * DIFF_SUB_DIM ** 0.5
             * jnp.max(jnp.abs(gq_ref[...]), axis=1, keepdims=True)
             * jnp.max(jnp.abs(gk_ref[...]), axis=1, keepdims=True))
    shift = reach + hi
    for d in range(N_BAND):
        o_ref[0, d] = tiles[d] - shift
    spread = 2.0 * reach + hi - lo
    stat_ref[0] = jnp.broadcast_to(jnp.where(spread > SHIFT_BOUND_LIMIT, 1.0, 0.0), (8, LANES))


def _bias_call(table, gq, gk):
    return pl.pallas_call(
        _bias_kernel,
        grid=(DIFF_HEADS,),
        in_specs=[pl.BlockSpec(memory_space=pltpu.SMEM), _resident((1, DIFF_SUB_DIM)),
                  _resident((1, DIFF_SUB_DIM))],
        out_specs=[pl.BlockSpec((1, N_BAND, KEY_TILE, Q_TILE), lambda h: (h, 0, 0, 0)),
                   pl.BlockSpec((1, 8, LANES), lambda h: (h, 0, 0))],
        out_shape=[jax.ShapeDtypeStruct((DIFF_HEADS, N_BAND, KEY_TILE, Q_TILE), F32),
                   jax.ShapeDtypeStruct((DIFF_HEADS, 8, LANES), F32)],
        compiler_params=_params(("arbitrary",)),
        name="rel_bias",
    )(table, gq, gk)


def _retention_kernel(dec_ref, q_ref, kt_ref, v_ref, gate_ref, o_ref):
    c = RET_CHUNK
    s_len = q_ref.shape[2]
    n_chunks = s_len // c
    dec = dec_ref[0]
    lg_f = jnp.log1p(-jnp.exp(dec[0:1]))
    lg_b = jnp.log1p(-jnp.exp(dec[1:2]))
    row_cc = lax.broadcasted_iota(jnp.int32, (c, c), 0)
    col_cc = lax.broadcasted_iota(jnp.int32, (c, c), 1)
    diff = (row_cc - col_cc).astype(F32)
    decay_mask = jnp.where(diff >= 0, jnp.exp(lg_f[:, :c] * diff), jnp.exp(lg_b[:, :c] * (-diff)))
    pos = lax.broadcasted_iota(jnp.int32, (1, c), 1).astype(F32)
    zeta_f = jnp.exp(lg_f[:, :c] * (c - 1.0 - pos))
    zeta_b = jnp.exp(lg_b[:, :c] * pos)
    row_v = lax.broadcasted_iota(jnp.int32, (c, RET_V_DIM), 0).astype(F32)
    xi_f = jnp.exp(lg_f * (row_v + 1.0))
    xi_b = jnp.exp(lg_b * (c - row_v))
    cd_f = jnp.exp(lg_f * c)
    cd_b = jnp.exp(lg_b * c)

    def rows(i):
        return slice(i * c, (i + 1) * c)

    kv = []
    for i in range(n_chunks):
        kt = kt_ref[0, :, rows(i)].astype(F32)
        kz = jnp.concatenate([kt * zeta_f, kt * zeta_b], axis=0).astype(BF16)
        kv.append(jnp.dot(kz, v_ref[0, 0, rows(i), :], preferred_element_type=F32))
    state = jnp.zeros((RET_QK_DIM, RET_V_DIM), F32)
    state_f = []
    for i in range(n_chunks):
        state_f.append(state.astype(BF16))
        state = state * cd_f + kv[i][:RET_QK_DIM]
    state = jnp.zeros((RET_QK_DIM, RET_V_DIM), F32)
    state_b = [None] * n_chunks
    for i in reversed(range(n_chunks)):
        state_b[i] = state.astype(BF16)
        state = state * cd_b + kv[i][RET_QK_DIM:]
    for i in range(n_chunks):
        q = q_ref[0, 0, rows(i), :]
        scores = jnp.dot(q, kt_ref[0, :, rows(i)], preferred_element_type=F32) * decay_mask
        o = (jnp.dot(scores.astype(BF16), v_ref[0, 0, rows(i), :], preferred_element_type=F32)
             + jnp.dot(q, state_f[i], preferred_element_type=F32) * xi_f
             + jnp.dot(q, state_b[i], preferred_element_type=F32) * xi_b)
        ms = jnp.mean(o * o, axis=-1, keepdims=True)
        gate = gate_ref[0, 0, rows(i), :].astype(F32)
        o_ref[0, 0, rows(i), :] = (gate * (o * lax.rsqrt(ms + EPS))).astype(BF16)


def _retention_call(dec, q, kt, v, gate, batch, s_len):
    def blk(width):
        return pl.BlockSpec((1, 1, s_len, width), lambda b, h: (h, b, 0, 0))

    t = batch * s_len
    r4 = lambda a: a.reshape(RET_HEADS, batch, s_len, a.shape[-1])
    out = pl.pallas_call(
        _retention_kernel,
        grid=(batch, RET_HEADS),
        in_specs=[pl.BlockSpec((1, 2, RET_V_DIM), lambda b, h: (h, 0, 0)),
                  blk(RET_QK_DIM), pl.BlockSpec((1, RET_QK_DIM, s_len), lambda b, h: (h, 0, b)),
                  blk(RET_V_DIM), blk(RET_V_DIM)],
        out_specs=blk(RET_V_DIM),
        out_shape=jax.ShapeDtypeStruct((RET_HEADS, batch, s_len, RET_V_DIM), BF16),
        compiler_params=_params(("parallel", "parallel")),
        name="retention",
    )(dec, r4(q), kt, r4(v), r4(gate))
    return out.reshape(RET_HEADS, t, RET_V_DIM)


def _diff_attn_kernel(flag_ref, lam_ref, qt_ref, k_ref, vt_ref, band_ref, g_ref, o_ref, *, lambda_init):
    head = pl.program_id(1)
    step = pl.program_id(2)
    s_len = k_ref.shape[2]
    n_kt = s_len // KEY_TILE
    sub = KEY_CHUNK // KEY_TILE
    lam_vec = lam_ref[...]
    lam = (jnp.exp(jnp.sum(lam_vec[0:1] * lam_vec[1:2], keepdims=True))
           - jnp.exp(jnp.sum(lam_vec[2:3] * lam_vec[3:4], keepdims=True)) + lambda_init)

    def query_maps(u):
        qt = qt_ref[0, :, u * Q_TILE:(u + 1) * Q_TILE]
        in_map0 = lax.broadcasted_iota(jnp.int32, qt.shape, 0) < DIFF_SUB_DIM
        zeros = jnp.zeros_like(qt)
        return jnp.concatenate([jnp.where(in_map0, qt, zeros), jnp.where(in_map0, zeros, qt)], axis=1)

    def bias_tile(t, qi):
        b = band_ref[0, jnp.clip(t - qi, -BAND, BAND) + BAND]
        return jnp.concatenate([b, b], axis=1)

    def finish(u, acc, denom):
        o2 = acc * (1.0 / denom)
        o_t = o2[:, :Q_TILE] - lam * o2[:, Q_TILE:]
        ms = jnp.mean(o_t * o_t, axis=0, keepdims=True)
        out_t = o_t * lax.rsqrt(ms + EPS) * g_ref[...] * (1.0 - lambda_init)
        o_ref[0, 0, u * Q_TILE:(u + 1) * Q_TILE, :] = out_t.T.astype(BF16)

    @pl.when(flag_ref[head] == 0)
    def _bounded():
        for u in range(Q_UNROLL):
            qi = step * Q_UNROLL + u
            q_maps = query_maps(u)
            acc = jnp.zeros((DIFF_V_DIM, 2 * Q_TILE), F32)
            denom = jnp.zeros((8, 2 * Q_TILE), F32)
            for c in range(s_len // KEY_CHUNK):
                rows = slice(c * KEY_CHUNK, (c + 1) * KEY_CHUNK)
                s = jnp.dot(k_ref[0, 0, rows, :], q_maps, preferred_element_type=F32)
                bias = jnp.concatenate([bias_tile(c * sub + j, qi) for j in range(sub)], axis=0)
                e = jnp.exp2(s + bias)
                denom = denom + jnp.sum(e.reshape(KEY_CHUNK // 8, 8, 2 * Q_TILE), axis=0)
                acc = acc + jnp.dot(vt_ref[0, :, rows], e.astype(BF16), preferred_element_type=F32)
            finish(u, acc, jnp.sum(denom, axis=0, keepdims=True))

    @pl.when(flag_ref[head] != 0)
    def _exact_maxima():
        for u in range(Q_UNROLL):
            qi = step * Q_UNROLL + u
            q_maps = query_maps(u)

            def logits(t):
                rows = pl.ds(pl.multiple_of(t * KEY_TILE, KEY_TILE), KEY_TILE)
                return jnp.dot(k_ref[0, 0, rows, :], q_maps, preferred_element_type=F32) + bias_tile(t, qi)

            def max_body(t, m):
                return jnp.maximum(m, jnp.max(logits(t), axis=0, keepdims=True))

            m = lax.fori_loop(0, n_kt, max_body, jnp.full((1, 2 * Q_TILE), jnp.finfo(F32).min, F32))

            def sum_body(t, carry):
                acc, denom = carry
                e = jnp.exp2(logits(t) - m)
                cols = pl.ds(pl.multiple_of(t * KEY_TILE, KEY_TILE), KEY_TILE)
                acc = acc + jnp.dot(vt_ref[0, :, cols], e.astype(BF16), preferred_element_type=F32)
                return acc, denom + jnp.sum(e, axis=0, keepdims=True)

            acc, denom = lax.fori_loop(
                0, n_kt, sum_body,
                (jnp.zeros((DIFF_V_DIM, 2 * Q_TILE), F32), jnp.zeros((1, 2 * Q_TILE), F32)))
            finish(u, acc, denom)


def _diff_attn_call(flags, lam_vec, qt, k, vt, band, g_bcast, batch, s_len, lambda_init):
    t = batch * s_len
    rows = Q_UNROLL * Q_TILE
    n_steps = s_len // rows
    out = pl.pallas_call(
        functools.partial(_diff_attn_kernel, lambda_init=lambda_init),
        grid=(batch, DIFF_HEADS, n_steps),
        in_specs=[pl.BlockSpec(memory_space=pltpu.SMEM), _resident((4, DIFF_SUB_DIM)),
                  pl.BlockSpec((1, DIFF_V_DIM, rows), lambda b, h, i: (h, 0, b * n_steps + i)),
                  pl.BlockSpec((1, 1, s_len, DIFF_V_DIM), lambda b, h, i: (h, b, 0, 0)),
                  pl.BlockSpec((1, DIFF_V_DIM, s_len), lambda b, h, i: (h, 0, b)),
                  pl.BlockSpec((1, N_BAND, KEY_TILE, Q_TILE), lambda b, h, i: (h, 0, 0, 0)),
                  _resident((DIFF_V_DIM, Q_TILE))],
        out_specs=pl.BlockSpec((1, 1, rows, DIFF_V_DIM), lambda b, h, i: (h, b, i, 0)),
        out_shape=jax.ShapeDtypeStruct((DIFF_HEADS, batch, s_len, DIFF_V_DIM), BF16),
        compiler_params=_params(("parallel", "parallel", "parallel")),
        name="diff_attn",
    )(flags, lam_vec, qt, k.reshape(DIFF_HEADS, batch, s_len, DIFF_V_DIM), vt, band, g_bcast)
    return out.reshape(DIFF_HEADS, t, DIFF_V_DIM)


def _output_kernel(x_ref, ret_ref, dif_ref, mg_ref, wo_ref, gf_ref, wg_ref, wu_ref, wd_ref, o_ref):
    ret = jnp.concatenate([ret_ref[h] for h in range(RET_HEADS)], axis=1).astype(F32)
    dif = jnp.concatenate([dif_ref[h] for h in range(DIFF_HEADS)], axis=1).astype(F32)
    mg = mg_ref[...].astype(F32)
    merged = (mg[:, :D_MODEL] * ret + mg[:, D_MODEL:] * dif).astype(BF16)
    x1 = x_ref[...] + jnp.dot(merged, wo_ref[...], preferred_element_type=F32)
    ms = jnp.mean(x1 * x1, axis=-1, keepdims=True)
    h = (x1 * lax.rsqrt(ms + EPS) * gf_ref[...]).astype(BF16)
    acc = x1
    ff = D_FF // FF_SPLIT
    for part in range(FF_SPLIT):
        cols = slice(part * ff, (part + 1) * ff)
        gate = jnp.dot(h, wg_ref[:, cols], preferred_element_type=F32)
        up = jnp.dot(h, wu_ref[:, cols], preferred_element_type=F32)
        act = (gate * _sigmoid(gate) * up).astype(BF16)
        acc = acc + jnp.dot(act, wd_ref[cols, :], preferred_element_type=F32)
    o_ref[...] = acc


def _output_call(x2, ret, dif, mg, wo, gf, wg, wu, wd):
    t, d = x2.shape
    tm = OUT_TM
    return pl.pallas_call(
        _output_kernel,
        grid=(t // tm,),
        in_specs=[pl.BlockSpec((tm, d), lambda i: (i, 0)),
                  pl.BlockSpec((RET_HEADS, tm, RET_V_DIM), lambda i: (0, i, 0)),
                  pl.BlockSpec((DIFF_HEADS, tm, DIFF_V_DIM), lambda i: (0, i, 0)),
                  pl.BlockSpec((tm, 2 * d), lambda i: (i, 0)),
                  _resident((d, d)), _resident((1, d)),
                  _resident((d, D_FF)), _resident((d, D_FF)), _resident((D_FF, d))],
        out_specs=pl.BlockSpec((tm, d), lambda i: (i, 0)),
        out_shape=jax.ShapeDtypeStruct((t, d), F32),
        compiler_params=_params(("parallel",)),
        name="merge_ffn",
    )(x2, ret, dif, mg, wo, gf, wg, wu, wd)


def _rotary_tables(s_len):
    half = RET_QK_DIM // 2
    inv = ROPE_BASE ** (-jnp.arange(half, dtype=F32) / half)
    ang = jnp.arange(s_len, dtype=F32)[:, None] * inv[None, :]
    cos, sin = jnp.cos(ang), jnp.sin(ang)
    return jnp.concatenate([cos, cos], axis=-1), jnp.concatenate([-sin, sin], axis=-1)


def _layer_weights(layer, norm_mix_g, w_in, ret_decay_fwd, ret_decay_bwd, q_norm_g, k_norm_g,
                   lam_q1, lam_k1, lam_q2, lam_k2, subln_g, w_out, norm_ffn_g, w_gate, w_up, w_down):
    blk = jnp.arange(MXU_DIM) // DIFF_SUB_DIM
    avg = jnp.where(blk[:, None] == blk[None, :], 1.0 / DIFF_SUB_DIM, 0.0).astype(BF16)
    dec = jnp.stack([ret_decay_fwd[layer], ret_decay_bwd[layer]], axis=1).astype(F32)
    return dict(
        w_in=w_in[layer].astype(BF16),
        norm_mix_g=norm_mix_g[layer].reshape(1, D_MODEL).astype(F32),
        avg=avg,
        q_gain=jnp.tile(q_norm_g[layer].astype(F32), D_MODEL // DIFF_SUB_DIM).reshape(1, D_MODEL),
        k_gain=jnp.tile(k_norm_g[layer].astype(F32), D_MODEL // DIFF_SUB_DIM).reshape(1, D_MODEL),
        dec=jnp.broadcast_to(dec[:, :, None], (RET_HEADS, 2, RET_V_DIM)),
        q_norm_g=q_norm_g[layer].astype(F32).reshape(1, DIFF_SUB_DIM),
        k_norm_g=k_norm_g[layer].astype(F32).reshape(1, DIFF_SUB_DIM),
        lam_vec=jnp.stack([lam_q1[layer], lam_k1[layer], lam_q2[layer], lam_k2[layer]]).astype(F32),
        subln=jnp.broadcast_to(subln_g[layer].astype(F32)[:, None], (DIFF_V_DIM, Q_TILE)),
        w_out=w_out[layer].astype(BF16),
        norm_ffn_g=norm_ffn_g[layer].reshape(1, D_MODEL).astype(F32),
        w_gate=w_gate[layer].astype(BF16), w_up=w_up[layer].astype(BF16),
        w_down=w_down[layer].astype(BF16),
    )


def _encoder_layer(x, layer, lw, bias, cos, sin):
    batch, s_len, d = x.shape
    t = batch * s_len
    x2 = x.reshape(t, d)
    rq, rkt, rv, rg, dqt, dk, dvt, mg = _in_proj_call(
        x2, lw["norm_mix_g"], lw["w_in"], cos, sin, lw["avg"], lw["q_gain"], lw["k_gain"], s_len)
    ret_out = _retention_call(lw["dec"], rq, rkt, rv, rg, batch, s_len)
    lambda_init = 0.8 - 0.6 * math.exp(-0.3 * layer)
    band, flags = bias
    dif_out = _diff_attn_call(flags, lw["lam_vec"], dqt, dk, dvt, band, lw["subln"], batch, s_len,
                              lambda_init)
    y2 = _output_call(x2, ret_out, dif_out, mg, lw["w_out"], lw["norm_ffn_g"],
                      lw["w_gate"], lw["w_up"], lw["w_down"])
    return y2.reshape(batch, s_len, d)


def kernel(x_prompt, x_sample, rel_bias_table, norm_mix_g, w_in, ret_decay_fwd, ret_decay_bwd,
           q_norm_g, k_norm_g, lam_q1, lam_k1, lam_q2, lam_k2, subln_g, w_out, norm_ffn_g,
           w_gate, w_up, w_down):
    table = rel_bias_table.astype(F32)
    layers = [_layer_weights(l, norm_mix_g, w_in, ret_decay_fwd, ret_decay_bwd, q_norm_g, k_norm_g,
                             lam_q1, lam_k1, lam_q2, lam_k2, subln_g, w_out, norm_ffn_g,
                             w_gate, w_up, w_down) for l in range(DEPTH)]
    biases = []
    for lw in layers:
        band, stat = _bias_call(table, lw["q_norm_g"], lw["k_norm_g"])
        biases.append((band, (stat[:, 0, 0] > 0.0).astype(jnp.int32)))
    outs = []
    for x in (x_prompt, x_sample):
        cos, sin = _rotary_tables(x.shape[1])
        for l in range(DEPTH):
            x = _encoder_layer(x, l, layers[l], biases[l], cos, sin)
        outs.append(x)
    return tuple(outs)
```

```python
import functools
import math

import jax
import jax.numpy as jnp
from jax import lax
from jax.experimental import pallas as pl
from jax.experimental.pallas import tpu as pltpu

D_MODEL = 1024
DEPTH = 1
RET_HEADS = 4
RET_V_DIM = D_MODEL // RET_HEADS
RET_QK_DIM = RET_V_DIM // 2
ROPE_BASE = 10000.0
DIFF_HEADS = 8
DIFF_V_DIM = D_MODEL // DIFF_HEADS
DIFF_SUB_DIM = DIFF_V_DIM // 2
N_BUCKETS = 32
MAX_DISTANCE = 128
D_FF = ((8 * D_MODEL // 3 + 255) // 256) * 256
EPS = 1e-6

LANES = 128
MXU_DIM = 256
VMEM_LIMIT_BYTES = 56 * 1024 * 1024

RET_CHUNK = 256
KEY_TILE = MXU_DIM
Q_TILE = MXU_DIM
KEY_CHUNK = 1024
Q_UNROLL = 8
REACH_SLACK = 1.02
BAND = 2
N_BAND = 2 * BAND + 1
LOG2E = math.log2(math.e)
SHIFT_BOUND_LIMIT = 80.0
PROJ_TM = 512
PROJ_ROW_SPLIT = 2
OUT_TM = 512
FF_SPLIT = 11

assert (BAND - 1) * KEY_TILE + 1 >= MAX_DISTANCE, "bias tiles outside the band must be constant"
assert KEY_TILE == Q_TILE
assert D_FF % (FF_SPLIT * LANES) == 0

F32 = jnp.float32
BF16 = jnp.bfloat16


def _params(semantics):
    return pltpu.CompilerParams(dimension_semantics=semantics, vmem_limit_bytes=VMEM_LIMIT_BYTES)


def _resident(shape):
    return pl.BlockSpec(shape, lambda *_: (0,) * len(shape), pipeline_mode=pl.Buffered(1))


def _sigmoid(y):
    return 1.0 / (1.0 + jnp.exp(-y))


_COL_SIZES = (RET_HEADS * RET_QK_DIM, RET_HEADS * RET_QK_DIM, RET_HEADS * RET_V_DIM,
              RET_HEADS * RET_V_DIM, DIFF_HEADS * DIFF_V_DIM, DIFF_HEADS * DIFF_V_DIM,
              DIFF_HEADS * DIFF_V_DIM, 2 * D_MODEL)
COL_RQ, COL_RK, COL_RV, COL_RG, COL_DQ, COL_DK, COL_DV, COL_MG = (
    sum(_COL_SIZES[:i]) for i in range(len(_COL_SIZES)))
IN_COLS = sum(_COL_SIZES)


def _in_proj_kernel(x_ref, g_ref, w_ref, cos_ref, sin_ref, avg_ref, gq_ref, gk_ref,
                    rq_ref, rkt_ref, rv_ref, rg_ref, dqt_ref, dk_ref, dvt_ref, mg_ref):
    n_rows = x_ref.shape[0] // PROJ_ROW_SPLIT
    half = RET_QK_DIM // 2
    for r in range(PROJ_ROW_SPLIT):
        rs = slice(r * n_rows, (r + 1) * n_rows)
        x = x_ref[rs, :]
        ms = jnp.mean(x * x, axis=-1, keepdims=True)
        h = (x * lax.rsqrt(ms + EPS) * g_ref[...]).astype(BF16)

        def project(start, size):
            return jnp.dot(h, w_ref[:, start:start + size], preferred_element_type=F32)

        def group_rms(y, gain, scale):
            yy = (y * y).astype(BF16)
            avg = avg_ref[...]
            ms = jnp.concatenate(
                [jnp.dot(yy[:, j * MXU_DIM:(j + 1) * MXU_DIM], avg, preferred_element_type=F32)
                 for j in range(y.shape[1] // MXU_DIM)], axis=1)
            yn = y * lax.rsqrt(ms + EPS) * gain
            return yn if scale == 1.0 else yn * scale

        cos = cos_ref[rs, :]
        sin = sin_ref[rs, :]
        y = project(COL_RQ, 2 * RET_HEADS * RET_QK_DIM)
        for hd in range(RET_HEADS):
            q = y[:, hd * RET_QK_DIM:(hd + 1) * RET_QK_DIM]
            k = y[:, COL_RK + hd * RET_QK_DIM:COL_RK + (hd + 1) * RET_QK_DIM]
            rq_ref[hd, rs, :] = (q * cos + pltpu.roll(q, half, 1) * sin).astype(BF16)
            rkt_ref[hd, :, rs] = ((k * cos + pltpu.roll(k, half, 1) * sin)
                                  * (RET_QK_DIM ** -0.5)).T.astype(BF16)
        y = project(COL_RV, RET_HEADS * RET_V_DIM)
        for hd in range(RET_HEADS):
            rv_ref[hd, rs, :] = y[:, hd * RET_V_DIM:(hd + 1) * RET_V_DIM].astype(BF16)
        y = project(COL_RG, RET_HEADS * RET_V_DIM)
        y = y * _sigmoid(y)
        for hd in range(RET_HEADS):
            rg_ref[hd, rs, :] = y[:, hd * RET_V_DIM:(hd + 1) * RET_V_DIM].astype(BF16)
        y = group_rms(project(COL_DQ, DIFF_HEADS * DIFF_V_DIM), gq_ref[...], DIFF_SUB_DIM ** -0.5 * LOG2E)
        for hd in range(DIFF_HEADS):
            dqt_ref[hd, :, rs] = y[:, hd * DIFF_V_DIM:(hd + 1) * DIFF_V_DIM].T.astype(BF16)
        y = group_rms(project(COL_DK, DIFF_HEADS * DIFF_V_DIM), gk_ref[...], 1.0)
        for hd in range(DIFF_HEADS):
            dk_ref[hd, rs, :] = y[:, hd * DIFF_V_DIM:(hd + 1) * DIFF_V_DIM].astype(BF16)
        y = project(COL_DV, DIFF_HEADS * DIFF_V_DIM)
        for hd in range(DIFF_HEADS):
            dvt_ref[hd, :, rs] = y[:, hd * DIFF_V_DIM:(hd + 1) * DIFF_V_DIM].T.astype(BF16)
        for part in range(2):
            y = project(COL_MG + part * D_MODEL, D_MODEL)
            mg_ref[rs, part * D_MODEL:(part + 1) * D_MODEL] = _sigmoid(y).astype(BF16)


def _in_proj_call(x2, g, w, cos, sin, avg, gq, gk, s_len):
    t, d = x2.shape
    tm = PROJ_TM

    def rows_major(heads, width):
        return (jax.ShapeDtypeStruct((heads, t, width), BF16),
                pl.BlockSpec((heads, tm, width), lambda i: (0, i, 0)))

    def cols_major(heads, width):
        return (jax.ShapeDtypeStruct((heads, width, t), BF16),
                pl.BlockSpec((heads, width, tm), lambda i: (0, 0, i)))

    outs = [rows_major(RET_HEADS, RET_QK_DIM), cols_major(RET_HEADS, RET_QK_DIM),
            rows_major(RET_HEADS, RET_V_DIM), rows_major(RET_HEADS, RET_V_DIM),
            cols_major(DIFF_HEADS, DIFF_V_DIM), rows_major(DIFF_HEADS, DIFF_V_DIM),
            cols_major(DIFF_HEADS, DIFF_V_DIM),
            (jax.ShapeDtypeStruct((t, 2 * d), BF16), pl.BlockSpec((tm, 2 * d), lambda i: (i, 0)))]
    pos_spec = pl.BlockSpec((tm, RET_QK_DIM), lambda i: (i % (s_len // tm), 0))
    return pl.pallas_call(
        _in_proj_kernel,
        grid=(t // tm,),
        in_specs=[pl.BlockSpec((tm, d), lambda i: (i, 0)), _resident((1, d)), _resident((d, IN_COLS)),
                  pos_spec, pos_spec, _resident((MXU_DIM, MXU_DIM)), _resident((1, d)), _resident((1, d))],
        out_specs=[o[1] for o in outs],
        out_shape=[o[0] for o in outs],
        compiler_params=_params(("parallel",)),
        name="in_proj",
    )(x2, g, w, cos, sin, avg, gq, gk)


def _bias_kernel(table_ref, gq_ref, gk_ref, o_ref, stat_ref):
    head = pl.program_id(0)
    nb = N_BUCKETS // 2
    max_exact = nb // 2
    j = lax.broadcasted_iota(jnp.int32, (KEY_TILE, Q_TILE), 0)
    i = lax.broadcasted_iota(jnp.int32, (KEY_TILE, Q_TILE), 1)
    tiles = []
    hi = lo = None
    for d in range(N_BAND):
        rel = (d - BAND) * KEY_TILE + j - i
        ret = jnp.where(rel > 0, nb, 0)
        n = jnp.abs(rel)
        nf = jnp.maximum(n, 1).astype(F32)
        large = max_exact + (jnp.log(nf / max_exact) / math.log(MAX_DISTANCE / max_exact)
                             * (nb - max_exact)).astype(jnp.int32)
        large = jnp.minimum(large, nb - 1)
        bucket = ret + jnp.where(n < max_exact, n, large)
        acc = jnp.zeros((KEY_TILE, Q_TILE), F32)
        for b in range(N_BUCKETS):
            acc = jnp.where(bucket == b, table_ref[b, head], acc)
        acc = acc * LOG2E
        tiles.append(acc)
        hi = acc if hi is None else jnp.maximum(hi, acc)
        lo = acc if lo is None else jnp.minimum(lo, acc)
    hi = jnp.max(jnp.max(hi, axis=0, keepdims=True), axis=1, keepdims=True)
    lo = jnp.min(jnp.min(lo, axis=0, keepdims=True), axis=1, keepdims=True)
    reach = (REACH_SLACK * LOG2E * DIFF_SUB_DIM ** 0.5
             * jnp.max(jnp.abs(gq_ref[...]), axis=1, keepdims=True)
             * jnp.max(jnp.abs(gk_ref[...]), axis=1, keepdims=True))
    shift = reach + hi
    for d in range(N_BAND):
        o_ref[0, d] = tiles[d] - shift
    spread = 2.0 * reach + hi - lo
    stat_ref[0] = jnp.broadcast_to(jnp.where(spread > SHIFT_BOUND_LIMIT, 1.0, 0.0), (8, LANES))


def _bias_call(table, gq, gk):
    return pl.pallas_call(
        _bias_kernel,
        grid=(DIFF_HEADS,),
        in_specs=[pl.BlockSpec(memory_space=pltpu.SMEM), _resident((1, DIFF_SUB_DIM)),
                  _resident((1, DIFF_SUB_DIM))],
        out_specs=[pl.BlockSpec((1, N_BAND, KEY_TILE, Q_TILE), lambda h: (h, 0, 0, 0)),
                   pl.BlockSpec((1, 8, LANES), lambda h: (h, 0, 0))],
        out_shape=[jax.ShapeDtypeStruct((DIFF_HEADS, N_BAND, KEY_TILE, Q_TILE), F32),
                   jax.ShapeDtypeStruct((DIFF_HEADS, 8, LANES), F32)],
        compiler_params=_params(("arbitrary",)),
        name="rel_bias",
    )(table, gq, gk)


def _retention_kernel(dec_ref, q_ref, kt_ref, v_ref, gate_ref, o_ref):
    c = RET_CHUNK
    s_len = q_ref.shape[2]
    n_chunks = s_len // c
    dec = dec_ref[0]
    lg_f = jnp.log1p(-jnp.exp(dec[0:1]))
    lg_b = jnp.log1p(-jnp.exp(dec[1:2]))
    row_cc = lax.broadcasted_iota(jnp.int32, (c, c), 0)
    col_cc = lax.broadcasted_iota(jnp.int32, (c, c), 1)
    diff = (row_cc - col_cc).astype(F32)
    decay_mask = jnp.where(diff >= 0, jnp.exp(lg_f[:, :c] * diff), jnp.exp(lg_b[:, :c] * (-diff)))
    pos = lax.broadcasted_iota(jnp.int32, (1, c), 1).astype(F32)
    zeta_f = jnp.exp(lg_f[:, :c] * (c - 1.0 - pos))
    zeta_b = jnp.exp(lg_b[:, :c] * pos)
    row_v = lax.broadcasted_iota(jnp.int32, (c, RET_V_DIM), 0).astype(F32)
    xi_f = jnp.exp(lg_f * (row_v + 1.0))
    xi_b = jnp.exp(lg_b * (c - row_v))
    cd_f = jnp.exp(lg_f * c)
    cd_b = jnp.exp(lg_b * c)

    def rows(i):
        return slice(i * c, (i + 1) * c)

    kv = []
    for i in range(n_chunks):
        kt = kt_ref[0, :, rows(i)].astype(F32)
        kz = jnp.concatenate([kt * zeta_f, kt * zeta_b], axis=0).astype(BF16)
        kv.append(jnp.dot(kz, v_ref[0, 0, rows(i), :], preferred_element_type=F32))
    state = jnp.zeros((RET_QK_DIM, RET_V_DIM), F32)
    state_f = []
    for i in range(n_chunks):
        state_f.append(state.astype(BF16))
        state = state * cd_f + kv[i][:RET_QK_DIM]
    state = jnp.zeros((RET_QK_DIM, RET_V_DIM), F32)
    state_b = [None] * n_chunks
    for i in reversed(range(n_chunks)):
        state_b[i] = state.astype(BF16)
        state = state * cd_b + kv[i][RET_QK_DIM:]
    for i in range(n_chunks):
        q = q_ref[0, 0, rows(i), :]
        scores = jnp.dot(q, kt_ref[0, :, rows(i)], preferred_element_type=F32) * decay_mask
        o = (jnp.dot(scores.astype(BF16), v_ref[0, 0, rows(i), :], preferred_element_type=F32)
             + jnp.dot(q, state_f[i], preferred_element_type=F32) * xi_f
             + jnp.dot(q, state_b[i], preferred_element_type=F32) * xi_b)
        ms = jnp.mean(o * o, axis=-1, keepdims=True)
        gate = gate_ref[0, 0, rows(i), :].astype(F32)
        o_ref[0, 0, rows(i), :] = (gate * (o * lax.rsqrt(ms + EPS))).astype(BF16)


def _retention_call(dec, q, kt, v, gate, batch, s_len):
    def blk(width):
        return pl.BlockSpec((1, 1, s_len, width), lambda b, h: (h, b, 0, 0))

    t = batch * s_len
    r4 = lambda a: a.reshape(RET_HEADS, batch, s_len, a.shape[-1])
    out = pl.pallas_call(
        _retention_kernel,
        grid=(batch, RET_HEADS),
        in_specs=[pl.BlockSpec((1, 2, RET_V_DIM), lambda b, h: (h, 0, 0)),
                  blk(RET_QK_DIM), pl.BlockSpec((1, RET_QK_DIM, s_len), lambda b, h: (h, 0, b)),
                  blk(RET_V_DIM), blk(RET_V_DIM)],
        out_specs=blk(RET_V_DIM),
        out_shape=jax.ShapeDtypeStruct((RET_HEADS, batch, s_len, RET_V_DIM), BF16),
        compiler_params=_params(("parallel", "parallel")),
        name="retention",
    )(dec, r4(q), kt, r4(v), r4(gate))
    return out.reshape(RET_HEADS, t, RET_V_DIM)


def _diff_attn_kernel(flag_ref, lam_ref, qt_ref, k_ref, vt_ref, band_ref, g_ref, o_ref, *, lambda_init):
    head = pl.program_id(0)
    step = pl.program_id(2)
    s_len = k_ref.shape[2]
    n_kt = s_len // KEY_TILE
    sub = KEY_CHUNK // KEY_TILE
    lam_vec = lam_ref[...]
    lam = (jnp.exp(jnp.sum(lam_vec[0:1] * lam_vec[1:2], keepdims=True))
           - jnp.exp(jnp.sum(lam_vec[2:3] * lam_vec[3:4], keepdims=True)) + lambda_init)

    def query_maps(u):
        qt = qt_ref[0, :, u * Q_TILE:(u + 1) * Q_TILE]
        in_map0 = lax.broadcasted_iota(jnp.int32, qt.shape, 0) < DIFF_SUB_DIM
        zeros = jnp.zeros_like(qt)
        return jnp.concatenate([jnp.where(in_map0, qt, zeros), jnp.where(in_map0, zeros, qt)], axis=1)

    def bias_tile(t, qi):
        b = band_ref[0, jnp.clip(t - qi, -BAND, BAND) + BAND]
        return jnp.concatenate([b, b], axis=1)

    def finish(u, acc, denom):
        o2 = acc * (1.0 / denom)
        o_t = o2[:, :Q_TILE] - lam * o2[:, Q_TILE:]
        ms = jnp.mean(o_t * o_t, axis=0, keepdims=True)
        out_t = o_t * lax.rsqrt(ms + EPS) * g_ref[...] * (1.0 - lambda_init)
        o_ref[0, 0, u * Q_TILE:(u + 1) * Q_TILE, :] = out_t.T.astype(BF16)

    @pl.when(flag_ref[head] == 0)
    def _bounded():
        for u in range(Q_UNROLL):
            qi = step * Q_UNROLL + u
            q_maps = query_maps(u)
            acc = jnp.zeros((DIFF_V_DIM, 2 * Q_TILE), F32)
            denom = jnp.zeros((8, 2 * Q_TILE), F32)
            for c in range(s_len // KEY_CHUNK):
                rows = slice(c * KEY_CHUNK, (c + 1) * KEY_CHUNK)
                s = jnp.dot(k_ref[0, 0, rows, :], q_maps, preferred_element_type=F32)
                bias = jnp.concatenate([bias_tile(c * sub + j, qi) for j in range(sub)], axis=0)
                e = jnp.exp2(s + bias)
                denom = denom + jnp.sum(e.reshape(KEY_CHUNK // 8, 8, 2 * Q_TILE), axis=0)
                acc = acc + jnp.dot(vt_ref[0, :, rows], e.astype(BF16), preferred_element_type=F32)
            finish(u, acc, jnp.sum(denom, axis=0, keepdims=True))

    @pl.when(flag_ref[head] != 0)
    def _exact_maxima():
        for u in range(Q_UNROLL):
            qi = step * Q_UNROLL + u
            q_maps = query_maps(u)

            def logits(t):
                rows = pl.ds(pl.multiple_of(t * KEY_TILE, KEY_TILE), KEY_TILE)
                return jnp.dot(k_ref[0, 0, rows, :], q_maps, preferred_element_type=F32) + bias_tile(t, qi)

            def max_body(t, m):
                return jnp.maximum(m, jnp.max(logits(t), axis=0, keepdims=True))

            m = lax.fori_loop(0, n_kt, max_body, jnp.full((1, 2 * Q_TILE), jnp.finfo(F32).min, F32))

            def sum_body(t, carry):
                acc, denom = carry
                e = jnp.exp2(logits(t) - m)
                cols = pl.ds(pl.multiple_of(t * KEY_TILE, KEY_TILE), KEY_TILE)
                acc = acc + jnp.dot(vt_ref[0, :, cols], e.astype(BF16), preferred_element_type=F32)
                return acc, denom + jnp.sum(e, axis=0, keepdims=True)

            acc, denom = lax.fori_loop(
                0, n_kt, sum_body,
                (jnp.zeros((DIFF_V_DIM, 2 * Q_TILE), F32), jnp.zeros((1, 2 * Q_TILE), F32)))
            finish(u, acc, denom)


def _diff_attn_call(flags, lam_vec, qt, k, vt, band, g_bcast, batch, s_len, lambda_init):
    t = batch * s_len
    rows = Q_UNROLL * Q_TILE
    n_steps = s_len // rows
    out = pl.pallas_call(
        functools.partial(_diff_attn_kernel, lambda_init=lambda_init),
        grid=(DIFF_HEADS, batch, n_steps),
        in_specs=[pl.BlockSpec(memory_space=pltpu.SMEM), _resident((4, DIFF_SUB_DIM)),
                  pl.BlockSpec((1, DIFF_V_DIM, rows), lambda h, b, i: (h, 0, b * n_steps + i)),
                  pl.BlockSpec((1, 1, s_len, DIFF_V_DIM), lambda h, b, i: (h, b, 0, 0)),
                  pl.BlockSpec((1, DIFF_V_DIM, s_len), lambda h, b, i: (h, 0, b)),
                  pl.BlockSpec((1, N_BAND, KEY_TILE, Q_TILE), lambda h, b, i: (h, 0, 0, 0)),
                  _resident((DIFF_V_DIM, Q_TILE))],
        out_specs=pl.BlockSpec((1, 1, rows, DIFF_V_DIM), lambda h, b, i: (h, b, i, 0)),
        out_shape=jax.ShapeDtypeStruct((DIFF_HEADS, batch, s_len, DIFF_V_DIM), BF16),
        compiler_params=_params(("parallel", "parallel", "parallel")),
        name="diff_attn",
    )(flags, lam_vec, qt, k.reshape(DIFF_HEADS, batch, s_len, DIFF_V_DIM), vt, band, g_bcast)
    return out.reshape(DIFF_HEADS, t, DIFF_V_DIM)


def _output_kernel(x_ref, ret_ref, dif_ref, mg_ref, wo_ref, gf_ref, wg_ref, wu_ref, wd_ref, o_ref):
    ret = jnp.concatenate([ret_ref[h] for h in range(RET_HEADS)], axis=1).astype(F32)
    dif = jnp.concatenate([dif_ref[h] for h in range(DIFF_HEADS)], axis=1).astype(F32)
    mg = mg_ref[...].astype(F32)
    merged = (mg[:, :D_MODEL] * ret + mg[:, D_MODEL:] * dif).astype(BF16)
    x1 = x_ref[...] + jnp.dot(merged, wo_ref[...], preferred_element_type=F32)
    ms = jnp.mean(x1 * x1, axis=-1, keepdims=True)
    h = (x1 * lax.rsqrt(ms + EPS) * gf_ref[...]).astype(BF16)
    acc = x1
    ff = D_FF // FF_SPLIT
    for part in range(FF_SPLIT):
        cols = slice(part * ff, (part + 1) * ff)
        gate = jnp.dot(h, wg_ref[:, cols], preferred_element_type=F32)
        up = jnp.dot(h, wu_ref[:, cols], preferred_element_type=F32)
        act = (gate * _sigmoid(gate) * up).astype(BF16)
        acc = acc + jnp.dot(act, wd_ref[cols, :], preferred_element_type=F32)
    o_ref[...] = acc


def _output_call(x2, ret, dif, mg, wo, gf, wg, wu, wd):
    t, d = x2.shape
    tm = OUT_TM
    return pl.pallas_call(
        _output_kernel,
        grid=(t // tm,),
        in_specs=[pl.BlockSpec((tm, d), lambda i: (i, 0)),
                  pl.BlockSpec((RET_HEADS, tm, RET_V_DIM), lambda i: (0, i, 0)),
                  pl.BlockSpec((DIFF_HEADS, tm, DIFF_V_DIM), lambda i: (0, i, 0)),
                  pl.BlockSpec((tm, 2 * d), lambda i: (i, 0)),
                  _resident((d, d)), _resident((1, d)),
                  _resident((d, D_FF)), _resident((d, D_FF)), _resident((D_FF, d))],
        out_specs=pl.BlockSpec((tm, d), lambda i: (i, 0)),
        out_shape=jax.ShapeDtypeStruct((t, d), F32),
        compiler_params=_params(("parallel",)),
        name="merge_ffn",
    )(x2, ret, dif, mg, wo, gf, wg, wu, wd)


def _rotary_tables(s_len):
    half = RET_QK_DIM // 2
    inv = ROPE_BASE ** (-jnp.arange(half, dtype=F32) / half)
    ang = jnp.arange(s_len, dtype=F32)[:, None] * inv[None, :]
    cos, sin = jnp.cos(ang), jnp.sin(ang)
    return jnp.concatenate([cos, cos], axis=-1), jnp.concatenate([-sin, sin], axis=-1)


def _layer_weights(layer, norm_mix_g, w_in, ret_decay_fwd, ret_decay_bwd, q_norm_g, k_norm_g,
                   lam_q1, lam_k1, lam_q2, lam_k2, subln_g, w_out, norm_ffn_g, w_gate, w_up, w_down):
    blk = jnp.arange(MXU_DIM) // DIFF_SUB_DIM
    avg = jnp.where(blk[:, None] == blk[None, :], 1.0 / DIFF_SUB_DIM, 0.0).astype(BF16)
    dec = jnp.stack([ret_decay_fwd[layer], ret_decay_bwd[layer]], axis=1).astype(F32)
    return dict(
        w_in=w_in[layer].astype(BF16),
        norm_mix_g=norm_mix_g[layer].reshape(1, D_MODEL).astype(F32),
        avg=avg,
        q_gain=jnp.tile(q_norm_g[layer].astype(F32), D_MODEL // DIFF_SUB_DIM).reshape(1, D_MODEL),
        k_gain=jnp.tile(k_norm_g[layer].astype(F32), D_MODEL // DIFF_SUB_DIM).reshape(1, D_MODEL),
        dec=jnp.broadcast_to(dec[:, :, None], (RET_HEADS, 2, RET_V_DIM)),
        q_norm_g=q_norm_g[layer].astype(F32).reshape(1, DIFF_SUB_DIM),
        k_norm_g=k_norm_g[layer].astype(F32).reshape(1, DIFF_SUB_DIM),
        lam_vec=jnp.stack([lam_q1[layer], lam_k1[layer], lam_q2[layer], lam_k2[layer]]).astype(F32),
        subln=jnp.broadcast_to(subln_g[layer].astype(F32)[:, None], (DIFF_V_DIM, Q_TILE)),
        w_out=w_out[layer].astype(BF16),
        norm_ffn_g=norm_ffn_g[layer].reshape(1, D_MODEL).astype(F32),
        w_gate=w_gate[layer].astype(BF16), w_up=w_up[layer].astype(BF16),
        w_down=w_down[layer].astype(BF16),
    )


def _encoder_layer(x, layer, lw, bias, cos, sin):
    batch, s_len, d = x.shape
    t = batch * s_len
    x2 = x.reshape(t, d)
    rq, rkt, rv, rg, dqt, dk, dvt, mg = _in_proj_call(
        x2, lw["norm_mix_g"], lw["w_in"], cos, sin, lw["avg"], lw["q_gain"], lw["k_gain"], s_len)
    ret_out = _retention_call(lw["dec"], rq, rkt, rv, rg, batch, s_len)
    lambda_init = 0.8 - 0.6 * math.exp(-0.3 * layer)
    band, flags = bias
    dif_out = _diff_attn_call(flags, lw["lam_vec"], dqt, dk, dvt, band, lw["subln"], batch, s_len,
                              lambda_init)
    y2 = _output_call(x2, ret_out, dif_out, mg, lw["w_out"], lw["norm_ffn_g"],
                      lw["w_gate"], lw["w_up"], lw["w_down"])
    return y2.reshape(batch, s_len, d)


def kernel(x_prompt, x_sample, rel_bias_table, norm_mix_g, w_in, ret_decay_fwd, ret_decay_bwd,
           q_norm_g, k_norm_g, lam_q1, lam_k1, lam_q2, lam_k2, subln_g, w_out, norm_ffn_g,
           w_gate, w_up, w_down):
    table = rel_bias_table.astype(F32)
    layers = [_layer_weights(l, norm_mix_g, w_in, ret_decay_fwd, ret_decay_bwd, q_norm_g, k_norm_g,
                             lam_q1, lam_k1, lam_q2, lam_k2, subln_g, w_out, norm_ffn_g,
                             w_gate, w_up, w_down) for l in range(DEPTH)]
    biases = []
    for lw in layers:
        band, stat = _bias_call(table, lw["q_norm_g"], lw["k_norm_g"])
        biases.append((band, (stat[:, 0, 0] > 0.0).astype(jnp.int32)))
    outs = []
    for x in (x_prompt, x_sample):
        cos, sin = _rotary_tables(x.shape[1])
        for l in range(DEPTH):
            x = _encoder_layer(x, l, layers[l], biases[l], cos, sin)
        outs.append(x)
    return tuple(outs)
```

```python
import functools
import math

import jax
import jax.numpy as jnp
from jax import lax
from jax.experimental import pallas as pl
from jax.experimental.pallas import tpu as pltpu

D_MODEL = 1024
DEPTH = 1
RET_HEADS = 4
RET_V_DIM = D_MODEL // RET_HEADS
RET_QK_DIM = RET_V_DIM // 2
ROPE_BASE = 10000.0
DIFF_HEADS = 8
DIFF_V_DIM = D_MODEL // DIFF_HEADS
DIFF_SUB_DIM = DIFF_V_DIM // 2
N_BUCKETS = 32
MAX_DISTANCE = 128
D_FF = ((8 * D_MODEL // 3 + 255) // 256) * 256
EPS = 1e-6

LANES = 128
MXU_DIM = 256
VMEM_LIMIT_BYTES = 56 * 1024 * 1024

RET_CHUNK = 256
KEY_TILE = MXU_DIM
Q_TILE = MXU_DIM
KEY_CHUNK = 1024
Q_UNROLL = 8
REACH_SLACK = 1.02
BAND = 2
N_BAND = 2 * BAND + 1
LOG2E = math.log2(math.e)
SHIFT_BOUND_LIMIT = 80.0
PROJ_TM = 512
PROJ_ROW_SPLIT = 2
OUT_TM = 512
FF_SPLIT = 11

assert (BAND - 1) * KEY_TILE + 1 >= MAX_DISTANCE, "bias tiles outside the band must be constant"
assert KEY_TILE == Q_TILE
assert D_FF % (FF_SPLIT * LANES) == 0

F32 = jnp.float32
BF16 = jnp.bfloat16


def _params(semantics):
    return pltpu.CompilerParams(dimension_semantics=semantics, vmem_limit_bytes=VMEM_LIMIT_BYTES)


def _resident(shape):
    return pl.BlockSpec(shape, lambda *_: (0,) * len(shape), pipeline_mode=pl.Buffered(1))


def _sigmoid(y):
    return 1.0 / (1.0 + jnp.exp(-y))


_COL_SIZES = (RET_HEADS * RET_QK_DIM, RET_HEADS * RET_QK_DIM, RET_HEADS * RET_V_DIM,
              RET_HEADS * RET_V_DIM, DIFF_HEADS * DIFF_V_DIM, DIFF_HEADS * DIFF_V_DIM,
              DIFF_HEADS * DIFF_V_DIM, 2 * D_MODEL)
COL_RQ, COL_RK, COL_RV, COL_RG, COL_DQ, COL_DK, COL_DV, COL_MG = (
    sum(_COL_SIZES[:i]) for i in range(len(_COL_SIZES)))
IN_COLS = sum(_COL_SIZES)


def _in_proj_kernel(x_ref, g_ref, w_ref, cos_ref, sin_ref, avg_ref, gqt_ref, gk_ref,
                    rq_ref, rkt_ref, rv_ref, rg_ref, dqt_ref, dk_ref, dvt_ref, mg_ref):
    n_rows = x_ref.shape[0] // PROJ_ROW_SPLIT
    half = RET_QK_DIM // 2
    for r in range(PROJ_ROW_SPLIT):
        rs = slice(r * n_rows, (r + 1) * n_rows)
        x = x_ref[rs, :]
        ms = jnp.mean(x * x, axis=-1, keepdims=True)
        h = (x * lax.rsqrt(ms + EPS) * g_ref[...]).astype(BF16)

        def project(start, size):
            return jnp.dot(h, w_ref[:, start:start + size], preferred_element_type=F32)

        def group_rms(y, gain):
            yy = (y * y).astype(BF16)
            avg = avg_ref[...]
            ms = jnp.concatenate(
                [jnp.dot(yy[:, j * MXU_DIM:(j + 1) * MXU_DIM], avg, preferred_element_type=F32)
                 for j in range(y.shape[1] // MXU_DIM)], axis=1)
            return y * lax.rsqrt(ms + EPS) * gain

        cos = cos_ref[rs, :]
        sin = sin_ref[rs, :]
        y = project(COL_RQ, 2 * RET_HEADS * RET_QK_DIM)
        for hd in range(RET_HEADS):
            q = y[:, hd * RET_QK_DIM:(hd + 1) * RET_QK_DIM]
            k = y[:, COL_RK + hd * RET_QK_DIM:COL_RK + (hd + 1) * RET_QK_DIM]
            rq_ref[hd, rs, :] = (q * cos + pltpu.roll(q, half, 1) * sin).astype(BF16)
            rkt_ref[hd, :, rs] = ((k * cos + pltpu.roll(k, half, 1) * sin)
                                  * (RET_QK_DIM ** -0.5)).T.astype(BF16)
        y = project(COL_RV, RET_HEADS * RET_V_DIM)
        for hd in range(RET_HEADS):
            rv_ref[hd, rs, :] = y[:, hd * RET_V_DIM:(hd + 1) * RET_V_DIM].astype(BF16)
        y = project(COL_RG, RET_HEADS * RET_V_DIM)
        y = y * _sigmoid(y)
        for hd in range(RET_HEADS):
            rg_ref[hd, rs, :] = y[:, hd * RET_V_DIM:(hd + 1) * RET_V_DIM].astype(BF16)
        y = project(COL_DQ, DIFF_HEADS * DIFF_V_DIM)
        gq_t = gqt_ref[...]
        in_map0 = lax.broadcasted_iota(jnp.int32, gq_t.shape, 0) < DIFF_SUB_DIM
        for hd in range(DIFF_HEADS):
            yt = y[:, hd * DIFF_V_DIM:(hd + 1) * DIFF_V_DIM].T
            sq = yt * yt
            ms0 = jnp.sum(sq[:DIFF_SUB_DIM], axis=0, keepdims=True) * (1.0 / DIFF_SUB_DIM)
            ms1 = jnp.sum(sq[DIFF_SUB_DIM:], axis=0, keepdims=True) * (1.0 / DIFF_SUB_DIM)
            inv = jnp.where(in_map0, lax.rsqrt(ms0 + EPS), lax.rsqrt(ms1 + EPS))
            dqt_ref[hd, :, rs] = (yt * inv * gq_t).astype(BF16)
        y = group_rms(project(COL_DK, DIFF_HEADS * DIFF_V_DIM), gk_ref[...])
        for hd in range(DIFF_HEADS):
            dk_ref[hd, rs, :] = y[:, hd * DIFF_V_DIM:(hd + 1) * DIFF_V_DIM].astype(BF16)
        y = project(COL_DV, DIFF_HEADS * DIFF_V_DIM)
        for hd in range(DIFF_HEADS):
            dvt_ref[hd, :, rs] = y[:, hd * DIFF_V_DIM:(hd + 1) * DIFF_V_DIM].T.astype(BF16)
        for part in range(2):
            y = project(COL_MG + part * D_MODEL, D_MODEL)
            mg_ref[rs, part * D_MODEL:(part + 1) * D_MODEL] = _sigmoid(y).astype(BF16)


def _in_proj_call(x2, g, w, cos, sin, avg, gq, gk, s_len):
    t, d = x2.shape
    tm = PROJ_TM

    def rows_major(heads, width):
        return (jax.ShapeDtypeStruct((heads, t, width), BF16),
                pl.BlockSpec((heads, tm, width), lambda i: (0, i, 0)))

    def cols_major(heads, width):
        return (jax.ShapeDtypeStruct((heads, width, t), BF16),
                pl.BlockSpec((heads, width, tm), lambda i: (0, 0, i)))

    outs = [rows_major(RET_HEADS, RET_QK_DIM), cols_major(RET_HEADS, RET_QK_DIM),
            rows_major(RET_HEADS, RET_V_DIM), rows_major(RET_HEADS, RET_V_DIM),
            cols_major(DIFF_HEADS, DIFF_V_DIM), rows_major(DIFF_HEADS, DIFF_V_DIM),
            cols_major(DIFF_HEADS, DIFF_V_DIM),
            (jax.ShapeDtypeStruct((t, 2 * d), BF16), pl.BlockSpec((tm, 2 * d), lambda i: (i, 0)))]
    pos_spec = pl.BlockSpec((tm, RET_QK_DIM), lambda i: (i % (s_len // tm), 0))
    return pl.pallas_call(
        _in_proj_kernel,
        grid=(t // tm,),
        in_specs=[pl.BlockSpec((tm, d), lambda i: (i, 0)), _resident((1, d)), _resident((d, IN_COLS)),
                  pos_spec, pos_spec, _resident((MXU_DIM, MXU_DIM)),
                  _resident((DIFF_V_DIM, tm // PROJ_ROW_SPLIT)), _resident((1, d))],
        out_specs=[o[1] for o in outs],
        out_shape=[o[0] for o in outs],
        compiler_params=_params(("parallel",)),
        name="in_proj",
    )(x2, g, w, cos, sin, avg, gq, gk)


def _bias_kernel(table_ref, gq_ref, gk_ref, o_ref, stat_ref):
    head = pl.program_id(0)
    nb = N_BUCKETS // 2
    max_exact = nb // 2
    j = lax.broadcasted_iota(jnp.int32, (KEY_TILE, Q_TILE), 0)
    i = lax.broadcasted_iota(jnp.int32, (KEY_TILE, Q_TILE), 1)
    tiles = []
    hi = lo = None
    for d in range(N_BAND):
        rel = (d - BAND) * KEY_TILE + j - i
        ret = jnp.where(rel > 0, nb, 0)
        n = jnp.abs(rel)
        nf = jnp.maximum(n, 1).astype(F32)
        large = max_exact + (jnp.log(nf / max_exact) / math.log(MAX_DISTANCE / max_exact)
                             * (nb - max_exact)).astype(jnp.int32)
        large = jnp.minimum(large, nb - 1)
        bucket = ret + jnp.where(n < max_exact, n, large)
        acc = jnp.zeros((KEY_TILE, Q_TILE), F32)
        for b in range(N_BUCKETS):
            acc = jnp.where(bucket == b, table_ref[b, head], acc)
        acc = acc * LOG2E
        tiles.append(acc)
        hi = acc if hi is None else jnp.maximum(hi, acc)
        lo = acc if lo is None else jnp.minimum(lo, acc)
    hi = jnp.max(jnp.max(hi, axis=0, keepdims=True), axis=1, keepdims=True)
    lo = jnp.min(jnp.min(lo, axis=0, keepdims=True), axis=1, keepdims=True)
    reach = (REACH_SLACK * LOG2E * DIFF_SUB_DIM ** 0.5
             * jnp.max(jnp.abs(gq_ref[...]), axis=1, keepdims=True)
             * jnp.max(jnp.abs(gk_ref[...]), axis=1, keepdims=True))
    shift = reach + hi
    for d in range(N_BAND):
        o_ref[0, d] = tiles[d] - shift
    spread = 2.0 * reach + hi - lo
    stat_ref[0] = jnp.broadcast_to(jnp.where(spread > SHIFT_BOUND_LIMIT, 1.0, 0.0), (8, LANES))


def _bias_call(table, gq, gk):
    return pl.pallas_call(
        _bias_kernel,
        grid=(DIFF_HEADS,),
        in_specs=[pl.BlockSpec(memory_space=pltpu.SMEM), _resident((1, DIFF_SUB_DIM)),
                  _resident((1, DIFF_SUB_DIM))],
        out_specs=[pl.BlockSpec((1, N_BAND, KEY_TILE, Q_TILE), lambda h: (h, 0, 0, 0)),
                   pl.BlockSpec((1, 8, LANES), lambda h: (h, 0, 0))],
        out_shape=[jax.ShapeDtypeStruct((DIFF_HEADS, N_BAND, KEY_TILE, Q_TILE), F32),
                   jax.ShapeDtypeStruct((DIFF_HEADS, 8, LANES), F32)],
        compiler_params=_params(("arbitrary",)),
        name="rel_bias",
    )(table, gq, gk)


def _retention_kernel(dec_ref, q_ref, kt_ref, v_ref, gate_ref, o_ref):
    c = RET_CHUNK
    s_len = q_ref.shape[2]
    n_chunks = s_len // c
    dec = dec_ref[0]
    lg_f = jnp.log1p(-jnp.exp(dec[0:1]))
    lg_b = jnp.log1p(-jnp.exp(dec[1:2]))
    row_cc = lax.broadcasted_iota(jnp.int32, (c, c), 0)
    col_cc = lax.broadcasted_iota(jnp.int32, (c, c), 1)
    diff = (row_cc - col_cc).astype(F32)
    decay_mask = jnp.where(diff >= 0, jnp.exp(lg_f[:, :c] * diff), jnp.exp(lg_b[:, :c] * (-diff)))
    pos = lax.broadcasted_iota(jnp.int32, (1, c), 1).astype(F32)
    zeta_f = jnp.exp(lg_f[:, :c] * (c - 1.0 - pos))
    zeta_b = jnp.exp(lg_b[:, :c] * pos)
    row_v = lax.broadcasted_iota(jnp.int32, (c, RET_V_DIM), 0).astype(F32)
    xi_f = jnp.exp(lg_f * (row_v + 1.0))
    xi_b = jnp.exp(lg_b * (c - row_v))
    cd_f = jnp.exp(lg_f * c)
    cd_b = jnp.exp(lg_b * c)

    def rows(i):
        return slice(i * c, (i + 1) * c)

    kv = []
    for i in range(n_chunks):
        kt = kt_ref[0, :, rows(i)].astype(F32)
        kz = jnp.concatenate([kt * zeta_f, kt * zeta_b], axis=0).astype(BF16)
        kv.append(jnp.dot(kz, v_ref[0, 0, rows(i), :], preferred_element_type=F32))
    state = jnp.zeros((RET_QK_DIM, RET_V_DIM), F32)
    state_f = []
    for i in range(n_chunks):
        state_f.append(state.astype(BF16))
        state = state * cd_f + kv[i][:RET_QK_DIM]
    state = jnp.zeros((RET_QK_DIM, RET_V_DIM), F32)
    state_b = [None] * n_chunks
    for i in reversed(range(n_chunks)):
        state_b[i] = state.astype(BF16)
        state = state * cd_b + kv[i][RET_QK_DIM:]
    for i in range(n_chunks):
        q = q_ref[0, 0, rows(i), :]
        scores = jnp.dot(q, kt_ref[0, :, rows(i)], preferred_element_type=F32) * decay_mask
        o = (jnp.dot(scores.astype(BF16), v_ref[0, 0, rows(i), :], preferred_element_type=F32)
             + jnp.dot(q, state_f[i], preferred_element_type=F32) * xi_f
             + jnp.dot(q, state_b[i], preferred_element_type=F32) * xi_b)
        ms = jnp.mean(o * o, axis=-1, keepdims=True)
        gate = gate_ref[0, 0, rows(i), :].astype(F32)
        o_ref[0, 0, rows(i), :] = (gate * (o * lax.rsqrt(ms + EPS))).astype(BF16)


def _retention_call(dec, q, kt, v, gate, batch, s_len):
    def blk(width):
        return pl.BlockSpec((1, 1, s_len, width), lambda b, h: (h, b, 0, 0))

    t = batch * s_len
    r4 = lambda a: a.reshape(RET_HEADS, batch, s_len, a.shape[-1])
    out = pl.pallas_call(
        _retention_kernel,
        grid=(batch, RET_HEADS),
        in_specs=[pl.BlockSpec((1, 2, RET_V_DIM), lambda b, h: (h, 0, 0)),
                  blk(RET_QK_DIM), pl.BlockSpec((1, RET_QK_DIM, s_len), lambda b, h: (h, 0, b)),
                  blk(RET_V_DIM), blk(RET_V_DIM)],
        out_specs=blk(RET_V_DIM),
        out_shape=jax.ShapeDtypeStruct((RET_HEADS, batch, s_len, RET_V_DIM), BF16),
        compiler_params=_params(("parallel", "parallel")),
        name="retention",
    )(dec, r4(q), kt, r4(v), r4(gate))
    return out.reshape(RET_HEADS, t, RET_V_DIM)


def _diff_attn_kernel(flag_ref, lam_ref, qt_ref, k_ref, vt_ref, band_ref, g_ref, o_ref, *, lambda_init):
    head = pl.program_id(0)
    step = pl.program_id(2)
    s_len = k_ref.shape[2]
    n_kt = s_len // KEY_TILE
    sub = KEY_CHUNK // KEY_TILE

    def learned_lambda():
        lam_vec = lam_ref[...]
        return (jnp.exp(jnp.sum(lam_vec[0:1] * lam_vec[1:2], keepdims=True))
                - jnp.exp(jnp.sum(lam_vec[2:3] * lam_vec[3:4], keepdims=True)) + lambda_init)

    def query_maps(u):
        qt = qt_ref[0, :, u * Q_TILE:(u + 1) * Q_TILE]
        in_map0 = lax.broadcasted_iota(jnp.int32, qt.shape, 0) < DIFF_SUB_DIM
        zeros = jnp.zeros_like(qt)
        return jnp.concatenate([jnp.where(in_map0, qt, zeros), jnp.where(in_map0, zeros, qt)], axis=1)

    def bias_tile(t, qi):
        b = band_ref[0, jnp.clip(t - qi, -BAND, BAND) + BAND]
        return jnp.concatenate([b, b], axis=1)

    def finish(u, acc, denom, lam):
        o2 = acc * (1.0 / denom)
        o_t = o2[:, :Q_TILE] - lam * o2[:, Q_TILE:]
        ms = jnp.mean(o_t * o_t, axis=0, keepdims=True)
        out_t = o_t * lax.rsqrt(ms + EPS) * g_ref[...] * (1.0 - lambda_init)
        o_ref[0, 0, u * Q_TILE:(u + 1) * Q_TILE, :] = out_t.T.astype(BF16)

    @pl.when(flag_ref[head] == 0)
    def _bounded():
        lam = learned_lambda()
        for u in range(Q_UNROLL):
            qi = step * Q_UNROLL + u
            q_maps = query_maps(u)
            acc = jnp.zeros((DIFF_V_DIM, 2 * Q_TILE), F32)
            denom = jnp.zeros((8, 2 * Q_TILE), F32)
            for c in range(s_len // KEY_CHUNK):
                rows = slice(c * KEY_CHUNK, (c + 1) * KEY_CHUNK)
                s = jnp.dot(k_ref[0, 0, rows, :], q_maps, preferred_element_type=F32)
                bias = jnp.concatenate([bias_tile(c * sub + j, qi) for j in range(sub)], axis=0)
                e = jnp.exp2(s + bias)
                denom = denom + jnp.sum(e.reshape(KEY_CHUNK // 8, 8, 2 * Q_TILE), axis=0)
                acc = acc + jnp.dot(vt_ref[0, :, rows], e.astype(BF16), preferred_element_type=F32)
            finish(u, acc, jnp.sum(denom, axis=0, keepdims=True), lam)

    @pl.when(flag_ref[head] != 0)
    def _exact_maxima():
        lam = learned_lambda()
        for u in range(Q_UNROLL):
            qi = step * Q_UNROLL + u
            q_maps = query_maps(u)

            def logits(t):
                rows = pl.ds(pl.multiple_of(t * KEY_TILE, KEY_TILE), KEY_TILE)
                return jnp.dot(k_ref[0, 0, rows, :], q_maps, preferred_element_type=F32) + bias_tile(t, qi)

            def max_body(t, m):
                return jnp.maximum(m, jnp.max(logits(t), axis=0, keepdims=True))

            m = lax.fori_loop(0, n_kt, max_body, jnp.full((1, 2 * Q_TILE), jnp.finfo(F32).min, F32))

            def sum_body(t, carry):
                acc, denom = carry
                e = jnp.exp2(logits(t) - m)
                cols = pl.ds(pl.multiple_of(t * KEY_TILE, KEY_TILE), KEY_TILE)
                acc = acc + jnp.dot(vt_ref[0, :, cols], e.astype(BF16), preferred_element_type=F32)
                return acc, denom + jnp.sum(e, axis=0, keepdims=True)

            acc, denom = lax.fori_loop(
                0, n_kt, sum_body,
                (jnp.zeros((DIFF_V_DIM, 2 * Q_TILE), F32), jnp.zeros((1, 2 * Q_TILE), F32)))
            finish(u, acc, denom, lam)


def _diff_attn_call(flags, lam_vec, qt, k, vt, band, g_bcast, batch, s_len, lambda_init):
    t = batch * s_len
    rows = Q_UNROLL * Q_TILE
    n_steps = s_len // rows
    out = pl.pallas_call(
        functools.partial(_diff_attn_kernel, lambda_init=lambda_init),
        grid=(DIFF_HEADS, batch, n_steps),
        in_specs=[pl.BlockSpec(memory_space=pltpu.SMEM), _resident((4, DIFF_SUB_DIM)),
                  pl.BlockSpec((1, DIFF_V_DIM, rows), lambda h, b, i: (h, 0, b * n_steps + i)),
                  pl.BlockSpec((1, 1, s_len, DIFF_V_DIM), lambda h, b, i: (h, b, 0, 0)),
                  pl.BlockSpec((1, DIFF_V_DIM, s_len), lambda h, b, i: (h, 0, b)),
                  pl.BlockSpec((1, N_BAND, KEY_TILE, Q_TILE), lambda h, b, i: (h, 0, 0, 0)),
                  _resident((DIFF_V_DIM, Q_TILE))],
        out_specs=pl.BlockSpec((1, 1, rows, DIFF_V_DIM), lambda h, b, i: (h, b, i, 0)),
        out_shape=jax.ShapeDtypeStruct((DIFF_HEADS, batch, s_len, DIFF_V_DIM), BF16),
        compiler_params=_params(("parallel", "parallel", "parallel")),
        name="diff_attn",
    )(flags, lam_vec, qt, k.reshape(DIFF_HEADS, batch, s_len, DIFF_V_DIM), vt, band, g_bcast)
    return out.reshape(DIFF_HEADS, t, DIFF_V_DIM)


def _output_kernel(x_ref, ret_ref, dif_ref, mg_ref, wo_ref, gf_ref, wg_ref, wu_ref, wd_ref, o_ref):
    ret = jnp.concatenate([ret_ref[h] for h in range(RET_HEADS)], axis=1).astype(F32)
    dif = jnp.concatenate([dif_ref[h] for h in range(DIFF_HEADS)], axis=1).astype(F32)
    mg = mg_ref[...].astype(F32)
    merged = (mg[:, :D_MODEL] * ret + mg[:, D_MODEL:] * dif).astype(BF16)
    x1 = x_ref[...] + jnp.dot(merged, wo_ref[...], preferred_element_type=F32)
    ms = jnp.mean(x1 * x1, axis=-1, keepdims=True)
    h = (x1 * lax.rsqrt(ms + EPS) * gf_ref[...]).astype(BF16)
    acc = x1
    ff = D_FF // FF_SPLIT
    for part in range(FF_SPLIT):
        cols = slice(part * ff, (part + 1) * ff)
        gate = jnp.dot(h, wg_ref[:, cols], preferred_element_type=F32)
        up = jnp.dot(h, wu_ref[:, cols], preferred_element_type=F32)
        act = (gate * _sigmoid(gate) * up).astype(BF16)
        acc = acc + jnp.dot(act, wd_ref[cols, :], preferred_element_type=F32)
    o_ref[...] = acc


def _output_call(x2, ret, dif, mg, wo, gf, wg, wu, wd):
    t, d = x2.shape
    tm = OUT_TM
    return pl.pallas_call(
        _output_kernel,
        grid=(t // tm,),
        in_specs=[pl.BlockSpec((tm, d), lambda i: (i, 0)),
                  pl.BlockSpec((RET_HEADS, tm, RET_V_DIM), lambda i: (0, i, 0)),
                  pl.BlockSpec((DIFF_HEADS, tm, DIFF_V_DIM), lambda i: (0, i, 0)),
                  pl.BlockSpec((tm, 2 * d), lambda i: (i, 0)),
                  _resident((d, d)), _resident((1, d)),
                  _resident((d, D_FF)), _resident((d, D_FF)), _resident((D_FF, d))],
        out_specs=pl.BlockSpec((tm, d), lambda i: (i, 0)),
        out_shape=jax.ShapeDtypeStruct((t, d), F32),
        compiler_params=_params(("parallel",)),
        name="merge_ffn",
    )(x2, ret, dif, mg, wo, gf, wg, wu, wd)


def _rotary_tables(s_len):
    half = RET_QK_DIM // 2
    inv = ROPE_BASE ** (-jnp.arange(half, dtype=F32) / half)
    ang = jnp.arange(s_len, dtype=F32)[:, None] * inv[None, :]
    cos, sin = jnp.cos(ang), jnp.sin(ang)
    return jnp.concatenate([cos, cos], axis=-1), jnp.concatenate([-sin, sin], axis=-1)


def _layer_weights(layer, norm_mix_g, w_in, ret_decay_fwd, ret_decay_bwd, q_norm_g, k_norm_g,
                   lam_q1, lam_k1, lam_q2, lam_k2, subln_g, w_out, norm_ffn_g, w_gate, w_up, w_down):
    blk = jnp.arange(MXU_DIM) // DIFF_SUB_DIM
    avg = jnp.where(blk[:, None] == blk[None, :], 1.0 / DIFF_SUB_DIM, 0.0).astype(BF16)
    dec = jnp.stack([ret_decay_fwd[layer], ret_decay_bwd[layer]], axis=1).astype(F32)
    return dict(
        w_in=w_in[layer].astype(BF16),
        norm_mix_g=norm_mix_g[layer].reshape(1, D_MODEL).astype(F32),
        avg=avg,
        q_gain=jnp.broadcast_to(
            (jnp.tile(q_norm_g[layer].astype(F32), 2) * (DIFF_SUB_DIM ** -0.5 * LOG2E))[:, None],
            (DIFF_V_DIM, PROJ_TM // PROJ_ROW_SPLIT)),
        k_gain=jnp.tile(k_norm_g[layer].astype(F32), D_MODEL // DIFF_SUB_DIM).reshape(1, D_MODEL),
        dec=jnp.broadcast_to(dec[:, :, None], (RET_HEADS, 2, RET_V_DIM)),
        q_norm_g=q_norm_g[layer].astype(F32).reshape(1, DIFF_SUB_DIM),
        k_norm_g=k_norm_g[layer].astype(F32).reshape(1, DIFF_SUB_DIM),
        lam_vec=jnp.stack([lam_q1[layer], lam_k1[layer], lam_q2[layer], lam_k2[layer]]).astype(F32),
        subln=jnp.broadcast_to(subln_g[layer].astype(F32)[:, None], (DIFF_V_DIM, Q_TILE)),
        w_out=w_out[layer].astype(BF16),
        norm_ffn_g=norm_ffn_g[layer].reshape(1, D_MODEL).astype(F32),
        w_gate=w_gate[layer].astype(BF16), w_up=w_up[layer].astype(BF16),
        w_down=w_down[layer].astype(BF16),
    )


def _encoder_layer(x, layer, lw, bias, cos, sin):
    batch, s_len, d = x.shape
    t = batch * s_len
    x2 = x.reshape(t, d)
    rq, rkt, rv, rg, dqt, dk, dvt, mg = _in_proj_call(
        x2, lw["norm_mix_g"], lw["w_in"], cos, sin, lw["avg"], lw["q_gain"], lw["k_gain"], s_len)
    ret_out = _retention_call(lw["dec"], rq, rkt, rv, rg, batch, s_len)
    lambda_init = 0.8 - 0.6 * math.exp(-0.3 * layer)
    band, flags = bias
    dif_out = _diff_attn_call(flags, lw["lam_vec"], dqt, dk, dvt, band, lw["subln"], batch, s_len,
                              lambda_init)
    y2 = _output_call(x2, ret_out, dif_out, mg, lw["w_out"], lw["norm_ffn_g"],
                      lw["w_gate"], lw["w_up"], lw["w_down"])
    return y2.reshape(batch, s_len, d)


def kernel(x_prompt, x_sample, rel_bias_table, norm_mix_g, w_in, ret_decay_fwd, ret_decay_bwd,
           q_norm_g, k_norm_g, lam_q1, lam_k1, lam_q2, lam_k2, subln_g, w_out, norm_ffn_g,
           w_gate, w_up, w_down):
    table = rel_bias_table.astype(F32)
    layers = [_layer_weights(l, norm_mix_g, w_in, ret_decay_fwd, ret_decay_bwd, q_norm_g, k_norm_g,
                             lam_q1, lam_k1, lam_q2, lam_k2, subln_g, w_out, norm_ffn_g,
                             w_gate, w_up, w_down) for l in range(DEPTH)]
    biases = []
    for lw in layers:
        band, stat = _bias_call(table, lw["q_norm_g"], lw["k_norm_g"])
        biases.append((band, (stat[:, 0, 0] > 0.0).astype(jnp.int32)))
    outs = []
    for x in (x_prompt, x_sample):
        cos, sin = _rotary_tables(x.shape[1])
        for l in range(DEPTH):
            x = _encoder_layer(x, l, layers[l], biases[l], cos, sin)
        outs.append(x)
    return tuple(outs)
```

```python
import functools
import math

import jax
import jax.numpy as jnp
from jax import lax
from jax.experimental import pallas as pl
from jax.experimental.pallas import tpu as pltpu

D_MODEL = 1024
DEPTH = 1
RET_HEADS = 4
RET_V_DIM = D_MODEL // RET_HEADS
RET_QK_DIM = RET_V_DIM // 2
ROPE_BASE = 10000.0
DIFF_HEADS = 8
DIFF_V_DIM = D_MODEL // DIFF_HEADS
DIFF_SUB_DIM = DIFF_V_DIM // 2
N_BUCKETS = 32
MAX_DISTANCE = 128
D_FF = ((8 * D_MODEL // 3 + 255) // 256) * 256
EPS = 1e-6

LANES = 128
MXU_DIM = 256
VMEM_LIMIT_BYTES = 56 * 1024 * 1024

RET_CHUNK = 256
RET_HEADS_PER_STEP = 4
KEY_TILE = MXU_DIM
Q_TILE = MXU_DIM
KEY_CHUNK = 1024
SEQ_PER_STEP = 2
REACH_SLACK = 1.02
BAND = 2
N_BAND = 2 * BAND + 1
LOG2E = math.log2(math.e)
SHIFT_BOUND_LIMIT = 80.0
PROJ_TM = 512
PROJ_ROW_SPLIT = 2
OUT_TM = 512
FF_SPLIT = 11

assert (BAND - 1) * KEY_TILE + 1 >= MAX_DISTANCE, "bias tiles outside the band must be constant"
assert KEY_TILE == Q_TILE
assert D_FF % (FF_SPLIT * LANES) == 0

F32 = jnp.float32
BF16 = jnp.bfloat16


def _params(semantics):
    return pltpu.CompilerParams(dimension_semantics=semantics, vmem_limit_bytes=VMEM_LIMIT_BYTES)


def _resident(shape):
    return pl.BlockSpec(shape, lambda *_: (0,) * len(shape), pipeline_mode=pl.Buffered(1))


def _sigmoid(y):
    return 1.0 / (1.0 + jnp.exp(-y))


_COL_SIZES = (RET_HEADS * RET_QK_DIM, RET_HEADS * RET_QK_DIM, RET_HEADS * RET_V_DIM,
              RET_HEADS * RET_V_DIM, DIFF_HEADS * DIFF_V_DIM, DIFF_HEADS * DIFF_V_DIM,
              DIFF_HEADS * DIFF_V_DIM, 2 * D_MODEL)
COL_RQ, COL_RK, COL_RV, COL_RG, COL_DQ, COL_DK, COL_DV, COL_MG = (
    sum(_COL_SIZES[:i]) for i in range(len(_COL_SIZES)))
IN_COLS = sum(_COL_SIZES)


def _in_proj_kernel(x_ref, g_ref, w_ref, cos_ref, sin_ref, avg_ref, gqt_ref, gk_ref,
                    rq_ref, rkt_ref, rv_ref, rg_ref, dqt_ref, dk_ref, dvt_ref, mg_ref):
    n_rows = x_ref.shape[0] // PROJ_ROW_SPLIT
    half = RET_QK_DIM // 2
    for r in range(PROJ_ROW_SPLIT):
        rs = slice(r * n_rows, (r + 1) * n_rows)
        x = x_ref[rs, :]
        ms = jnp.mean(x * x, axis=-1, keepdims=True)
        h = (x * lax.rsqrt(ms + EPS) * g_ref[...]).astype(BF16)

        def project(start, size):
            return jnp.dot(h, w_ref[:, start:start + size], preferred_element_type=F32)

        def group_rms(y, gain):
            yy = (y * y).astype(BF16)
            avg = avg_ref[...]
            ms = jnp.concatenate(
                [jnp.dot(yy[:, j * MXU_DIM:(j + 1) * MXU_DIM], avg, preferred_element_type=F32)
                 for j in range(y.shape[1] // MXU_DIM)], axis=1)
            return y * lax.rsqrt(ms + EPS) * gain

        cos = cos_ref[rs, :]
        sin = sin_ref[rs, :]
        y = project(COL_RQ, 2 * RET_HEADS * RET_QK_DIM)
        for hd in range(RET_HEADS):
            q = y[:, hd * RET_QK_DIM:(hd + 1) * RET_QK_DIM]
            k = y[:, COL_RK + hd * RET_QK_DIM:COL_RK + (hd + 1) * RET_QK_DIM]
            rq_ref[hd, rs, :] = (q * cos + pltpu.roll(q, half, 1) * sin).astype(BF16)
            rkt_ref[hd, :, rs] = ((k * cos + pltpu.roll(k, half, 1) * sin)
                                  * (RET_QK_DIM ** -0.5)).T.astype(BF16)
        y = project(COL_RV, RET_HEADS * RET_V_DIM)
        for hd in range(RET_HEADS):
            rv_ref[hd, rs, :] = y[:, hd * RET_V_DIM:(hd + 1) * RET_V_DIM].astype(BF16)
        y = project(COL_RG, RET_HEADS * RET_V_DIM)
        y = y * _sigmoid(y)
        for hd in range(RET_HEADS):
            rg_ref[hd, rs, :] = y[:, hd * RET_V_DIM:(hd + 1) * RET_V_DIM].astype(BF16)
        y = project(COL_DQ, DIFF_HEADS * DIFF_V_DIM)
        gq_t = gqt_ref[...]
        in_map0 = lax.broadcasted_iota(jnp.int32, gq_t.shape, 0) < DIFF_SUB_DIM
        for hd in range(DIFF_HEADS):
            yt = y[:, hd * DIFF_V_DIM:(hd + 1) * DIFF_V_DIM].T
            sq = yt * yt
            ms0 = jnp.sum(sq[:DIFF_SUB_DIM], axis=0, keepdims=True) * (1.0 / DIFF_SUB_DIM)
            ms1 = jnp.sum(sq[DIFF_SUB_DIM:], axis=0, keepdims=True) * (1.0 / DIFF_SUB_DIM)
            inv = jnp.where(in_map0, lax.rsqrt(ms0 + EPS), lax.rsqrt(ms1 + EPS))
            dqt_ref[hd, :, rs] = (yt * inv * gq_t).astype(BF16)
        y = group_rms(project(COL_DK, DIFF_HEADS * DIFF_V_DIM), gk_ref[...])
        for hd in range(DIFF_HEADS):
            dk_ref[hd, rs, :] = y[:, hd * DIFF_V_DIM:(hd + 1) * DIFF_V_DIM].astype(BF16)
        y = project(COL_DV, DIFF_HEADS * DIFF_V_DIM)
        for hd in range(DIFF_HEADS):
            dvt_ref[hd, :, rs] = y[:, hd * DIFF_V_DIM:(hd + 1) * DIFF_V_DIM].T.astype(BF16)
        for part in range(2):
            y = project(COL_MG + part * D_MODEL, D_MODEL)
            mg_ref[rs, part * D_MODEL:(part + 1) * D_MODEL] = _sigmoid(y).astype(BF16)


def _in_proj_call(x2, g, w, cos, sin, avg, gq, gk, s_len):
    t, d = x2.shape
    tm = PROJ_TM

    def rows_major(heads, width):
        return (jax.ShapeDtypeStruct((heads, t, width), BF16),
                pl.BlockSpec((heads, tm, width), lambda i: (0, i, 0)))

    def cols_major(heads, width):
        return (jax.ShapeDtypeStruct((heads, width, t), BF16),
                pl.BlockSpec((heads, width, tm), lambda i: (0, 0, i)))

    outs = [rows_major(RET_HEADS, RET_QK_DIM), cols_major(RET_HEADS, RET_QK_DIM),
            rows_major(RET_HEADS, RET_V_DIM), rows_major(RET_HEADS, RET_V_DIM),
            cols_major(DIFF_HEADS, DIFF_V_DIM), rows_major(DIFF_HEADS, DIFF_V_DIM),
            cols_major(DIFF_HEADS, DIFF_V_DIM),
            (jax.ShapeDtypeStruct((t, 2 * d), BF16), pl.BlockSpec((tm, 2 * d), lambda i: (i, 0)))]
    pos_spec = pl.BlockSpec((tm, RET_QK_DIM), lambda i: (i % (s_len // tm), 0))
    return pl.pallas_call(
        _in_proj_kernel,
        grid=(t // tm,),
        in_specs=[pl.BlockSpec((tm, d), lambda i: (i, 0)), _resident((1, d)), _resident((d, IN_COLS)),
                  pos_spec, pos_spec, _resident((MXU_DIM, MXU_DIM)),
                  _resident((DIFF_V_DIM, tm // PROJ_ROW_SPLIT)), _resident((1, d))],
        out_specs=[o[1] for o in outs],
        out_shape=[o[0] for o in outs],
        compiler_params=_params(("parallel",)),
        name="in_proj",
    )(x2, g, w, cos, sin, avg, gq, gk)


def _bias_kernel(table_ref, gq_ref, gk_ref, o_ref, stat_ref):
    head = pl.program_id(0)
    nb = N_BUCKETS // 2
    max_exact = nb // 2
    j = lax.broadcasted_iota(jnp.int32, (KEY_TILE, Q_TILE), 0)
    i = lax.broadcasted_iota(jnp.int32, (KEY_TILE, Q_TILE), 1)
    tiles = []
    hi = lo = None
    for d in range(N_BAND):
        rel = (d - BAND) * KEY_TILE + j - i
        ret = jnp.where(rel > 0, nb, 0)
        n = jnp.abs(rel)
        nf = jnp.maximum(n, 1).astype(F32)
        large = max_exact + (jnp.log(nf / max_exact) / math.log(MAX_DISTANCE / max_exact)
                             * (nb - max_exact)).astype(jnp.int32)
        large = jnp.minimum(large, nb - 1)
        bucket = ret + jnp.where(n < max_exact, n, large)
        acc = jnp.zeros((KEY_TILE, Q_TILE), F32)
        for b in range(N_BUCKETS):
            acc = jnp.where(bucket == b, table_ref[b, head], acc)
        acc = acc * LOG2E
        tiles.append(acc)
        hi = acc if hi is None else jnp.maximum(hi, acc)
        lo = acc if lo is None else jnp.minimum(lo, acc)
    hi = jnp.max(jnp.max(hi, axis=0, keepdims=True), axis=1, keepdims=True)
    lo = jnp.min(jnp.min(lo, axis=0, keepdims=True), axis=1, keepdims=True)
    reach = (REACH_SLACK * LOG2E * DIFF_SUB_DIM ** 0.5
             * jnp.max(jnp.abs(gq_ref[...]), axis=1, keepdims=True)
             * jnp.max(jnp.abs(gk_ref[...]), axis=1, keepdims=True))
    shift = reach + hi
    for d in range(N_BAND):
        o_ref[0, d] = tiles[d] - shift
    spread = 2.0 * reach + hi - lo
    stat_ref[0] = jnp.broadcast_to(jnp.where(spread > SHIFT_BOUND_LIMIT, 1.0, 0.0), (8, LANES))


def _bias_call(table, gq, gk):
    return pl.pallas_call(
        _bias_kernel,
        grid=(DIFF_HEADS,),
        in_specs=[pl.BlockSpec(memory_space=pltpu.SMEM), _resident((1, DIFF_SUB_DIM)),
                  _resident((1, DIFF_SUB_DIM))],
        out_specs=[pl.BlockSpec((1, N_BAND, KEY_TILE, Q_TILE), lambda h: (h, 0, 0, 0)),
                   pl.BlockSpec((1, 8, LANES), lambda h: (h, 0, 0))],
        out_shape=[jax.ShapeDtypeStruct((DIFF_HEADS, N_BAND, KEY_TILE, Q_TILE), F32),
                   jax.ShapeDtypeStruct((DIFF_HEADS, 8, LANES), F32)],
        compiler_params=_params(("arbitrary",)),
        name="rel_bias",
    )(table, gq, gk)


def _retention_kernel(dec_ref, q_ref, kt_ref, v_ref, gate_ref, o_ref):
    c = RET_CHUNK
    s_len = q_ref.shape[2]
    n_chunks = s_len // c
    row_cc = lax.broadcasted_iota(jnp.int32, (c, c), 0)
    col_cc = lax.broadcasted_iota(jnp.int32, (c, c), 1)
    diff = (row_cc - col_cc).astype(F32)
    pos = lax.broadcasted_iota(jnp.int32, (1, c), 1).astype(F32)
    row_v = lax.broadcasted_iota(jnp.int32, (c, RET_V_DIM), 0).astype(F32)

    def rows(i):
        return slice(i * c, (i + 1) * c)

    for hd in range(RET_HEADS_PER_STEP):
        dec = dec_ref[hd]
        lg_f = jnp.log1p(-jnp.exp(dec[0:1]))
        lg_b = jnp.log1p(-jnp.exp(dec[1:2]))
        decay_mask = jnp.where(diff >= 0, jnp.exp(lg_f[:, :c] * diff), jnp.exp(lg_b[:, :c] * (-diff)))
        zeta_f = jnp.exp(lg_f[:, :c] * (c - 1.0 - pos))
        zeta_b = jnp.exp(lg_b[:, :c] * pos)
        xi_f = jnp.exp(lg_f * (row_v + 1.0))
        xi_b = jnp.exp(lg_b * (c - row_v))
        cd_f = jnp.exp(lg_f * c)
        cd_b = jnp.exp(lg_b * c)

        kv = []
        for i in range(n_chunks):
            kt = kt_ref[hd, :, rows(i)].astype(F32)
            kz = jnp.concatenate([kt * zeta_f, kt * zeta_b], axis=0).astype(BF16)
            kv.append(jnp.dot(kz, v_ref[hd, 0, rows(i), :], preferred_element_type=F32))
        state = jnp.zeros((RET_QK_DIM, RET_V_DIM), F32)
        state_f = []
        for i in range(n_chunks):
            state_f.append(state.astype(BF16))
            state = state * cd_f + kv[i][:RET_QK_DIM]
        state = jnp.zeros((RET_QK_DIM, RET_V_DIM), F32)
        state_b = [None] * n_chunks
        for i in reversed(range(n_chunks)):
            state_b[i] = state.astype(BF16)
            state = state * cd_b + kv[i][RET_QK_DIM:]
        for i in range(n_chunks):
            q = q_ref[hd, 0, rows(i), :]
            scores = jnp.dot(q, kt_ref[hd, :, rows(i)], preferred_element_type=F32) * decay_mask
            o = (jnp.dot(scores.astype(BF16), v_ref[hd, 0, rows(i), :], preferred_element_type=F32)
                 + jnp.dot(q, state_f[i], preferred_element_type=F32) * xi_f
                 + jnp.dot(q, state_b[i], preferred_element_type=F32) * xi_b)
            ms = jnp.mean(o * o, axis=-1, keepdims=True)
            gate = gate_ref[hd, 0, rows(i), :].astype(F32)
            o_ref[hd, 0, rows(i), :] = (gate * (o * lax.rsqrt(ms + EPS))).astype(BF16)


def _retention_call(dec, q, kt, v, gate, batch, s_len):
    hp = RET_HEADS_PER_STEP

    def blk(width):
        return pl.BlockSpec((hp, 1, s_len, width), lambda b, h: (h, b, 0, 0))

    t = batch * s_len
    r4 = lambda a: a.reshape(RET_HEADS, batch, s_len, a.shape[-1])
    out = pl.pallas_call(
        _retention_kernel,
        grid=(batch, RET_HEADS // hp),
        in_specs=[pl.BlockSpec((hp, 2, RET_V_DIM), lambda b, h: (h, 0, 0)),
                  blk(RET_QK_DIM), pl.BlockSpec((hp, RET_QK_DIM, s_len), lambda b, h: (h, 0, b)),
                  blk(RET_V_DIM), blk(RET_V_DIM)],
        out_specs=blk(RET_V_DIM),
        out_shape=jax.ShapeDtypeStruct((RET_HEADS, batch, s_len, RET_V_DIM), BF16),
        compiler_params=_params(("parallel", "parallel")),
        name="retention",
    )(dec, r4(q), kt, r4(v), r4(gate))
    return out.reshape(RET_HEADS, t, RET_V_DIM)


def _diff_attn_kernel(flag_ref, lam_ref, qt_ref, k_ref, vt_ref, band_ref, g_ref, o_ref, *, lambda_init):
    head = pl.program_id(0)
    s_len = k_ref.shape[2]
    n_kt = s_len // KEY_TILE
    n_qt = s_len // Q_TILE
    sub = KEY_CHUNK // KEY_TILE

    def learned_lambda():
        lam_vec = lam_ref[...]
        return (jnp.exp(jnp.sum(lam_vec[0:1] * lam_vec[1:2], keepdims=True))
                - jnp.exp(jnp.sum(lam_vec[2:3] * lam_vec[3:4], keepdims=True)) + lambda_init)

    def query_maps(j, u):
        start = j * s_len + u * Q_TILE
        cols = pl.ds(start if isinstance(u, int) else pl.multiple_of(start, Q_TILE), Q_TILE)
        qt = qt_ref[0, :, cols]
        in_map0 = lax.broadcasted_iota(jnp.int32, qt.shape, 0) < DIFF_SUB_DIM
        zeros = jnp.zeros_like(qt)
        return jnp.concatenate([jnp.where(in_map0, qt, zeros), jnp.where(in_map0, zeros, qt)], axis=1)

    def bias_tile(t, qi):
        b = band_ref[0, jnp.clip(t - qi, -BAND, BAND) + BAND]
        return jnp.concatenate([b, b], axis=1)

    def finish(j, u, acc, denom, lam):
        o2 = acc * (1.0 / denom)
        o_t = o2[:, :Q_TILE] - lam * o2[:, Q_TILE:]
        ms = jnp.mean(o_t * o_t, axis=0, keepdims=True)
        out_t = o_t * lax.rsqrt(ms + EPS) * g_ref[...] * (1.0 - lambda_init)
        start = u * Q_TILE
        rows = pl.ds(start if isinstance(u, int) else pl.multiple_of(start, Q_TILE), Q_TILE)
        o_ref[0, j, rows, :] = out_t.T.astype(BF16)

    @pl.when(flag_ref[head] == 0)
    def _bounded():
        lam = learned_lambda()
        for j in range(SEQ_PER_STEP):
            for u in range(n_qt):
                q_maps = query_maps(j, u)
                acc = jnp.zeros((DIFF_V_DIM, 2 * Q_TILE), F32)
                denom = jnp.zeros((8, 2 * Q_TILE), F32)
                for c in range(s_len // KEY_CHUNK):
                    rows = slice(c * KEY_CHUNK, (c + 1) * KEY_CHUNK)
                    cols = slice(j * s_len + c * KEY_CHUNK, j * s_len + (c + 1) * KEY_CHUNK)
                    s = jnp.dot(k_ref[0, j, rows, :], q_maps, preferred_element_type=F32)
                    bias = jnp.concatenate([bias_tile(c * sub + jj, u) for jj in range(sub)], axis=0)
                    e = jnp.exp2(s + bias)
                    denom = denom + jnp.sum(e.reshape(KEY_CHUNK // 8, 8, 2 * Q_TILE), axis=0)
                    acc = acc + jnp.dot(vt_ref[0, :, cols], e.astype(BF16), preferred_element_type=F32)
                finish(j, u, acc, jnp.sum(denom, axis=0, keepdims=True), lam)

    @pl.when(flag_ref[head] != 0)
    def _exact_maxima():
        lam = learned_lambda()
        for j in range(SEQ_PER_STEP):
            def tile_body(u, carry):
                q_maps = query_maps(j, u)

                def logits(t):
                    rows = pl.ds(pl.multiple_of(t * KEY_TILE, KEY_TILE), KEY_TILE)
                    return jnp.dot(k_ref[0, j, rows, :], q_maps, preferred_element_type=F32) + bias_tile(t, u)

                def max_body(t, m):
                    return jnp.maximum(m, jnp.max(logits(t), axis=0, keepdims=True))

                m = lax.fori_loop(0, n_kt, max_body, jnp.full((1, 2 * Q_TILE), jnp.finfo(F32).min, F32))

                def sum_body(t, inner):
                    acc, denom = inner
                    e = jnp.exp2(logits(t) - m)
                    cols = pl.ds(pl.multiple_of(j * s_len + t * KEY_TILE, KEY_TILE), KEY_TILE)
                    acc = acc + jnp.dot(vt_ref[0, :, cols], e.astype(BF16), preferred_element_type=F32)
                    return acc, denom + jnp.sum(e, axis=0, keepdims=True)

                acc, denom = lax.fori_loop(
                    0, n_kt, sum_body,
                    (jnp.zeros((DIFF_V_DIM, 2 * Q_TILE), F32), jnp.zeros((1, 2 * Q_TILE), F32)))
                finish(j, u, acc, denom, lam)
                return carry

            lax.fori_loop(0, n_qt, tile_body, 0)


def _diff_attn_call(flags, lam_vec, qt, k, vt, band, g_bcast, batch, s_len, lambda_init):
    t = batch * s_len
    sp = SEQ_PER_STEP
    head_t = pl.BlockSpec((1, DIFF_V_DIM, sp * s_len), lambda h, b: (h, 0, b))
    head_r = pl.BlockSpec((1, sp, s_len, DIFF_V_DIM), lambda h, b: (h, b, 0, 0))
    out = pl.pallas_call(
        functools.partial(_diff_attn_kernel, lambda_init=lambda_init),
        grid=(DIFF_HEADS, batch // sp),
        in_specs=[pl.BlockSpec(memory_space=pltpu.SMEM), _resident((4, DIFF_SUB_DIM)),
                  head_t, head_r, head_t,
                  pl.BlockSpec((1, N_BAND, KEY_TILE, Q_TILE), lambda h, b: (h, 0, 0, 0)),
                  _resident((DIFF_V_DIM, Q_TILE))],
        out_specs=head_r,
        out_shape=jax.ShapeDtypeStruct((DIFF_HEADS, batch, s_len, DIFF_V_DIM), BF16),
        compiler_params=_params(("parallel", "parallel")),
        name="diff_attn",
    )(flags, lam_vec, qt, k.reshape(DIFF_HEADS, batch, s_len, DIFF_V_DIM), vt, band, g_bcast)
    return out.reshape(DIFF_HEADS, t, DIFF_V_DIM)


def _output_kernel(x_ref, ret_ref, dif_ref, mg_ref, wo_ref, gf_ref, wg_ref, wu_ref, wd_ref, o_ref):
    ret = jnp.concatenate([ret_ref[h] for h in range(RET_HEADS)], axis=1).astype(F32)
    dif = jnp.concatenate([dif_ref[h] for h in range(DIFF_HEADS)], axis=1).astype(F32)
    mg = mg_ref[...].astype(F32)
    merged = (mg[:, :D_MODEL] * ret + mg[:, D_MODEL:] * dif).astype(BF16)
    x1 = x_ref[...] + jnp.dot(merged, wo_ref[...], preferred_element_type=F32)
    ms = jnp.mean(x1 * x1, axis=-1, keepdims=True)
    h = (x1 * lax.rsqrt(ms + EPS) * gf_ref[...]).astype(BF16)
    acc = x1
    ff = D_FF // FF_SPLIT
    for part in range(FF_SPLIT):
        cols = slice(part * ff, (part + 1) * ff)
        gate = jnp.dot(h, wg_ref[:, cols], preferred_element_type=F32)
        up = jnp.dot(h, wu_ref[:, cols], preferred_element_type=F32)
        act = (gate * _sigmoid(gate) * up).astype(BF16)
        acc = acc + jnp.dot(act, wd_ref[cols, :], preferred_element_type=F32)
    o_ref[...] = acc


def _output_call(x2, ret, dif, mg, wo, gf, wg, wu, wd):
    t, d = x2.shape
    tm = OUT_TM
    return pl.pallas_call(
        _output_kernel,
        grid=(t // tm,),
        in_specs=[pl.BlockSpec((tm, d), lambda i: (i, 0)),
                  pl.BlockSpec((RET_HEADS, tm, RET_V_DIM), lambda i: (0, i, 0)),
                  pl.BlockSpec((DIFF_HEADS, tm, DIFF_V_DIM), lambda i: (0, i, 0)),
                  pl.BlockSpec((tm, 2 * d), lambda i: (i, 0)),
                  _resident((d, d)), _resident((1, d)),
                  _resident((d, D_FF)), _resident((d, D_FF)), _resident((D_FF, d))],
        out_specs=pl.BlockSpec((tm, d), lambda i: (i, 0)),
        out_shape=jax.ShapeDtypeStruct((t, d), F32),
        compiler_params=_params(("parallel",)),
        name="merge_ffn",
    )(x2, ret, dif, mg, wo, gf, wg, wu, wd)


def _rotary_tables(s_len):
    half = RET_QK_DIM // 2
    inv = ROPE_BASE ** (-jnp.arange(half, dtype=F32) / half)
    ang = jnp.arange(s_len, dtype=F32)[:, None] * inv[None, :]
    cos, sin = jnp.cos(ang), jnp.sin(ang)
    return jnp.concatenate([cos, cos], axis=-1), jnp.concatenate([-sin, sin], axis=-1)


def _layer_weights(layer, norm_mix_g, w_in, ret_decay_fwd, ret_decay_bwd, q_norm_g, k_norm_g,
                   lam_q1, lam_k1, lam_q2, lam_k2, subln_g, w_out, norm_ffn_g, w_gate, w_up, w_down):
    blk = jnp.arange(MXU_DIM) // DIFF_SUB_DIM
    avg = jnp.where(blk[:, None] == blk[None, :], 1.0 / DIFF_SUB_DIM, 0.0).astype(BF16)
    dec = jnp.stack([ret_decay_fwd[layer], ret_decay_bwd[layer]], axis=1).astype(F32)
    return dict(
        w_in=w_in[layer].astype(BF16),
        norm_mix_g=norm_mix_g[layer].reshape(1, D_MODEL).astype(F32),
        avg=avg,
        q_gain=jnp.broadcast_to(
            (jnp.tile(q_norm_g[layer].astype(F32), 2) * (DIFF_SUB_DIM ** -0.5 * LOG2E))[:, None],
            (DIFF_V_DIM, PROJ_TM // PROJ_ROW_SPLIT)),
        k_gain=jnp.tile(k_norm_g[layer].astype(F32), D_MODEL // DIFF_SUB_DIM).reshape(1, D_MODEL),
        dec=jnp.broadcast_to(dec[:, :, None], (RET_HEADS, 2, RET_V_DIM)),
        q_norm_g=q_norm_g[layer].astype(F32).reshape(1, DIFF_SUB_DIM),
        k_norm_g=k_norm_g[layer].astype(F32).reshape(1, DIFF_SUB_DIM),
        lam_vec=jnp.stack([lam_q1[layer], lam_k1[layer], lam_q2[layer], lam_k2[layer]]).astype(F32),
        subln=jnp.broadcast_to(subln_g[layer].astype(F32)[:, None], (DIFF_V_DIM, Q_TILE)),
        w_out=w_out[layer].astype(BF16),
        norm_ffn_g=norm_ffn_g[layer].reshape(1, D_MODEL).astype(F32),
        w_gate=w_gate[layer].astype(BF16), w_up=w_up[layer].astype(BF16),
        w_down=w_down[layer].astype(BF16),
    )


def _encoder_layer(x, layer, lw, bias, cos, sin):
    batch, s_len, d = x.shape
    t = batch * s_len
    assert d == D_MODEL and s_len % PROJ_TM == 0 and s_len % KEY_CHUNK == 0 and s_len % RET_CHUNK == 0
    assert t % OUT_TM == 0 and batch % SEQ_PER_STEP == 0 and RET_HEADS % RET_HEADS_PER_STEP == 0
    x2 = x.reshape(t, d)
    rq, rkt, rv, rg, dqt, dk, dvt, mg = _in_proj_call(
        x2, lw["norm_mix_g"], lw["w_in"], cos, sin, lw["avg"], lw["q_gain"], lw["k_gain"], s_len)
    ret_out = _retention_call(lw["dec"], rq, rkt, rv, rg, batch, s_len)
    lambda_init = 0.8 - 0.6 * math.exp(-0.3 * layer)
    band, flags = bias
    dif_out = _diff_attn_call(flags, lw["lam_vec"], dqt, dk, dvt, band, lw["subln"], batch, s_len,
                              lambda_init)
    y2 = _output_call(x2, ret_out, dif_out, mg, lw["w_out"], lw["norm_ffn_g"],
                      lw["w_gate"], lw["w_up"], lw["w_down"])
    return y2.reshape(batch, s_len, d)


def kernel(x_prompt, x_sample, rel_bias_table, norm_mix_g, w_in, ret_decay_fwd, ret_decay_bwd,
           q_norm_g, k_norm_g, lam_q1, lam_k1, lam_q2, lam_k2, subln_g, w_out, norm_ffn_g,
           w_gate, w_up, w_down):
    table = rel_bias_table.astype(F32)
    layers = [_layer_weights(l, norm_mix_g, w_in, ret_decay_fwd, ret_decay_bwd, q_norm_g, k_norm_g,
                             lam_q1, lam_k1, lam_q2, lam_k2, subln_g, w_out, norm_ffn_g,
                             w_gate, w_up, w_down) for l in range(DEPTH)]
    biases = []
    for lw in layers:
        band, stat = _bias_call(table, lw["q_norm_g"], lw["k_norm_g"])
        biases.append((band, (stat[:, 0, 0] > 0.0).astype(jnp.int32)))
    outs = []
    for x in (x_prompt, x_sample):
        cos, sin = _rotary_tables(x.shape[1])
        for l in range(DEPTH):
            x = _encoder_layer(x, l, layers[l], biases[l], cos, sin)
        outs.append(x)
    return tuple(outs)
```

```python
import functools
import math

import jax
import jax.numpy as jnp
from jax import lax
from jax.experimental import pallas as pl
from jax.experimental.pallas import tpu as pltpu

D_MODEL = 1024
DEPTH = 1
RET_HEADS = 4
RET_V_DIM = D_MODEL // RET_HEADS
RET_QK_DIM = RET_V_DIM // 2
ROPE_BASE = 10000.0
DIFF_HEADS = 8
DIFF_V_DIM = D_MODEL // DIFF_HEADS
DIFF_SUB_DIM = DIFF_V_DIM // 2
N_BUCKETS = 32
MAX_DISTANCE = 128
D_FF = ((8 * D_MODEL // 3 + 255) // 256) * 256
EPS = 1e-6

LANES = 128
MXU_DIM = 256
VMEM_LIMIT_BYTES = 56 * 1024 * 1024

RET_CHUNK = 256
RET_HEADS_PER_STEP = 4
KEY_TILE = MXU_DIM
Q_TILE = MXU_DIM
KEY_CHUNK = 1024
SEQ_PER_STEP = 2
REACH_SLACK = 1.02
BAND = 2
N_BAND = 2 * BAND + 1
LOG2E = math.log2(math.e)
SHIFT_BOUND_LIMIT = 80.0
PROJ_TM = 512
PROJ_ROW_SPLIT = 2
OUT_TM = 512
FF_SPLIT = 11

assert (BAND - 1) * KEY_TILE + 1 >= MAX_DISTANCE, "bias tiles outside the band must be constant"
assert KEY_TILE == Q_TILE
assert D_FF % (FF_SPLIT * LANES) == 0

F32 = jnp.float32
BF16 = jnp.bfloat16


def _params(semantics):
    return pltpu.CompilerParams(dimension_semantics=semantics, vmem_limit_bytes=VMEM_LIMIT_BYTES)


def _resident(shape):
    return pl.BlockSpec(shape, lambda *_: (0,) * len(shape), pipeline_mode=pl.Buffered(1))


def _sigmoid(y):
    return 1.0 / (1.0 + jnp.exp(-y))


_COL_SIZES = (RET_HEADS * RET_QK_DIM, RET_HEADS * RET_QK_DIM, RET_HEADS * RET_V_DIM,
              RET_HEADS * RET_V_DIM, DIFF_HEADS * DIFF_V_DIM, DIFF_HEADS * DIFF_V_DIM,
              DIFF_HEADS * DIFF_V_DIM, 2 * D_MODEL)
COL_RQ, COL_RK, COL_RV, COL_RG, COL_DQ, COL_DK, COL_DV, COL_MG = (
    sum(_COL_SIZES[:i]) for i in range(len(_COL_SIZES)))
IN_COLS = sum(_COL_SIZES)


def _in_proj_kernel(x_ref, g_ref, w_ref, cos_ref, sin_ref, avg_ref, gqt_ref, gk_ref,
                    rq_ref, rkt_ref, rv_ref, rg_ref, dqt_ref, dk_ref, dvt_ref, mg_ref):
    n_rows = x_ref.shape[0] // PROJ_ROW_SPLIT
    half = RET_QK_DIM // 2
    for r in range(PROJ_ROW_SPLIT):
        rs = slice(r * n_rows, (r + 1) * n_rows)
        x = x_ref[rs, :]
        ms = jnp.mean(x * x, axis=-1, keepdims=True)
        h = (x * lax.rsqrt(ms + EPS) * g_ref[...]).astype(BF16)

        def project(start, size):
            return jnp.dot(h, w_ref[:, start:start + size], preferred_element_type=F32)

        def group_rms(y, gain):
            yy = (y * y).astype(BF16)
            avg = avg_ref[...]
            ms = jnp.concatenate(
                [jnp.dot(yy[:, j * MXU_DIM:(j + 1) * MXU_DIM], avg, preferred_element_type=F32)
                 for j in range(y.shape[1] // MXU_DIM)], axis=1)
            return y * lax.rsqrt(ms + EPS) * gain

        cos = cos_ref[rs, :]
        sin = sin_ref[rs, :]
        y = project(COL_RQ, 2 * RET_HEADS * RET_QK_DIM)
        for hd in range(RET_HEADS):
            q = y[:, hd * RET_QK_DIM:(hd + 1) * RET_QK_DIM]
            k = y[:, COL_RK + hd * RET_QK_DIM:COL_RK + (hd + 1) * RET_QK_DIM]
            rq_ref[hd, rs, :] = (q * cos + pltpu.roll(q, half, 1) * sin).astype(BF16)
            rkt_ref[hd, :, rs] = ((k * cos + pltpu.roll(k, half, 1) * sin)
                                  * (RET_QK_DIM ** -0.5)).T.astype(BF16)
        y = project(COL_RG, RET_HEADS * RET_V_DIM)
        y = y * _sigmoid(y)
        for hd in range(RET_HEADS):
            rg_ref[hd, rs, :] = y[:, hd * RET_V_DIM:(hd + 1) * RET_V_DIM].astype(BF16)
        y = project(COL_DQ, DIFF_HEADS * DIFF_V_DIM)
        gq_t = gqt_ref[...]
        in_map0 = lax.broadcasted_iota(jnp.int32, gq_t.shape, 0) < DIFF_SUB_DIM
        for hd in range(DIFF_HEADS):
            yt = y[:, hd * DIFF_V_DIM:(hd + 1) * DIFF_V_DIM].T
            sq = yt * yt
            ms0 = jnp.sum(sq[:DIFF_SUB_DIM], axis=0, keepdims=True) * (1.0 / DIFF_SUB_DIM)
            ms1 = jnp.sum(sq[DIFF_SUB_DIM:], axis=0, keepdims=True) * (1.0 / DIFF_SUB_DIM)
            inv = jnp.where(in_map0, lax.rsqrt(ms0 + EPS), lax.rsqrt(ms1 + EPS))
            dqt_ref[hd, :, rs] = (yt * inv * gq_t).astype(BF16)
        y = group_rms(project(COL_DK, DIFF_HEADS * DIFF_V_DIM), gk_ref[...])
        for hd in range(DIFF_HEADS):
            dk_ref[hd, rs, :] = y[:, hd * DIFF_V_DIM:(hd + 1) * DIFF_V_DIM].astype(BF16)
        y = project(COL_DV, DIFF_HEADS * DIFF_V_DIM)
        for hd in range(DIFF_HEADS):
            dvt_ref[hd, :, rs] = y[:, hd * DIFF_V_DIM:(hd + 1) * DIFF_V_DIM].T.astype(BF16)
        for part in range(2):
            y = project(COL_MG + part * D_MODEL, D_MODEL)
            mg_ref[rs, part * D_MODEL:(part + 1) * D_MODEL] = _sigmoid(y).astype(BF16)
        y = project(COL_RV, RET_HEADS * RET_V_DIM)
        for hd in range(RET_HEADS):
            rv_ref[hd, rs, :] = y[:, hd * RET_V_DIM:(hd + 1) * RET_V_DIM].astype(BF16)


def _in_proj_call(x2, g, w, cos, sin, avg, gq, gk, s_len):
    t, d = x2.shape
    tm = PROJ_TM

    def rows_major(heads, width):
        return (jax.ShapeDtypeStruct((heads, t, width), BF16),
                pl.BlockSpec((heads, tm, width), lambda i: (0, i, 0)))

    def cols_major(heads, width):
        return (jax.ShapeDtypeStruct((heads, width, t), BF16),
                pl.BlockSpec((heads, width, tm), lambda i: (0, 0, i)))

    outs = [rows_major(RET_HEADS, RET_QK_DIM), cols_major(RET_HEADS, RET_QK_DIM),
            rows_major(RET_HEADS, RET_V_DIM), rows_major(RET_HEADS, RET_V_DIM),
            cols_major(DIFF_HEADS, DIFF_V_DIM), rows_major(DIFF_HEADS, DIFF_V_DIM),
            cols_major(DIFF_HEADS, DIFF_V_DIM),
            (jax.ShapeDtypeStruct((t, 2 * d), BF16), pl.BlockSpec((tm, 2 * d), lambda i: (i, 0)))]
    pos_spec = pl.BlockSpec((tm, RET_QK_DIM), lambda i: (i % (s_len // tm), 0))
    return pl.pallas_call(
        _in_proj_kernel,
        grid=(t // tm,),
        in_specs=[pl.BlockSpec((tm, d), lambda i: (i, 0)), _resident((1, d)), _resident((d, IN_COLS)),
                  pos_spec, pos_spec, _resident((MXU_DIM, MXU_DIM)),
                  _resident((DIFF_V_DIM, tm // PROJ_ROW_SPLIT)), _resident((1, d))],
        out_specs=[o[1] for o in outs],
        out_shape=[o[0] for o in outs],
        compiler_params=_params(("parallel",)),
        name="in_proj",
    )(x2, g, w, cos, sin, avg, gq, gk)


def _bias_kernel(table_ref, gq_ref, gk_ref, o_ref, stat_ref):
    head = pl.program_id(0)
    nb = N_BUCKETS // 2
    max_exact = nb // 2
    j = lax.broadcasted_iota(jnp.int32, (KEY_TILE, Q_TILE), 0)
    i = lax.broadcasted_iota(jnp.int32, (KEY_TILE, Q_TILE), 1)
    tiles = []
    hi = lo = None
    for d in range(N_BAND):
        rel = (d - BAND) * KEY_TILE + j - i
        ret = jnp.where(rel > 0, nb, 0)
        n = jnp.abs(rel)
        nf = jnp.maximum(n, 1).astype(F32)
        large = max_exact + (jnp.log(nf / max_exact) / math.log(MAX_DISTANCE / max_exact)
                             * (nb - max_exact)).astype(jnp.int32)
        large = jnp.minimum(large, nb - 1)
        bucket = ret + jnp.where(n < max_exact, n, large)
        acc = jnp.zeros((KEY_TILE, Q_TILE), F32)
        for b in range(N_BUCKETS):
            acc = jnp.where(bucket == b, table_ref[b, head], acc)
        acc = acc * LOG2E
        tiles.append(acc)
        hi = acc if hi is None else jnp.maximum(hi, acc)
        lo = acc if lo is None else jnp.minimum(lo, acc)
    hi = jnp.max(jnp.max(hi, axis=0, keepdims=True), axis=1, keepdims=True)
    lo = jnp.min(jnp.min(lo, axis=0, keepdims=True), axis=1, keepdims=True)
    reach = (REACH_SLACK * LOG2E * DIFF_SUB_DIM ** 0.5
             * jnp.max(jnp.abs(gq_ref[...]), axis=1, keepdims=True)
             * jnp.max(jnp.abs(gk_ref[...]), axis=1, keepdims=True))
    shift = reach + hi
    for d in range(N_BAND):
        o_ref[0, d] = tiles[d] - shift
    spread = 2.0 * reach + hi - lo
    stat_ref[0] = jnp.broadcast_to(jnp.where(spread > SHIFT_BOUND_LIMIT, 1.0, 0.0), (8, LANES))


def _bias_call(table, gq, gk):
    return pl.pallas_call(
        _bias_kernel,
        grid=(DIFF_HEADS,),
        in_specs=[pl.BlockSpec(memory_space=pltpu.SMEM), _resident((1, DIFF_SUB_DIM)),
                  _resident((1, DIFF_SUB_DIM))],
        out_specs=[pl.BlockSpec((1, N_BAND, KEY_TILE, Q_TILE), lambda h: (h, 0, 0, 0)),
                   pl.BlockSpec((1, 8, LANES), lambda h: (h, 0, 0))],
        out_shape=[jax.ShapeDtypeStruct((DIFF_HEADS, N_BAND, KEY_TILE, Q_TILE), F32),
                   jax.ShapeDtypeStruct((DIFF_HEADS, 8, LANES), F32)],
        compiler_params=_params(("arbitrary",)),
        name="rel_bias",
    )(table, gq, gk)


def _retention_kernel(dec_ref, q_ref, kt_ref, v_ref, gate_ref, o_ref):
    c = RET_CHUNK
    s_len = q_ref.shape[2]
    n_chunks = s_len // c
    row_cc = lax.broadcasted_iota(jnp.int32, (c, c), 0)
    col_cc = lax.broadcasted_iota(jnp.int32, (c, c), 1)
    diff = (row_cc - col_cc).astype(F32)
    pos = lax.broadcasted_iota(jnp.int32, (1, c), 1).astype(F32)
    row_v = lax.broadcasted_iota(jnp.int32, (c, RET_V_DIM), 0).astype(F32)

    def rows(i):
        return slice(i * c, (i + 1) * c)

    for hd in range(RET_HEADS_PER_STEP):
        dec = dec_ref[hd]
        lg_f = jnp.log1p(-jnp.exp(dec[0:1]))
        lg_b = jnp.log1p(-jnp.exp(dec[1:2]))
        decay_mask = jnp.where(diff >= 0, jnp.exp(lg_f[:, :c] * diff), jnp.exp(lg_b[:, :c] * (-diff)))
        zeta_f = jnp.exp(lg_f[:, :c] * (c - 1.0 - pos))
        zeta_b = jnp.exp(lg_b[:, :c] * pos)
        row_q = row_v[:, :RET_QK_DIM]
        xi_f = jnp.exp(lg_f[:, :RET_QK_DIM] * (row_q + 1.0))
        xi_b = jnp.exp(lg_b[:, :RET_QK_DIM] * (c - row_q))
        cd_f = jnp.exp(lg_f * c)
        cd_b = jnp.exp(lg_b * c)

        kv = []
        for i in range(n_chunks):
            kt = kt_ref[hd, :, rows(i)].astype(F32)
            kz = jnp.concatenate([kt * zeta_f, kt * zeta_b], axis=0).astype(BF16)
            kv.append(jnp.dot(kz, v_ref[hd, 0, rows(i), :], preferred_element_type=F32))
        state = jnp.zeros((RET_QK_DIM, RET_V_DIM), F32)
        state_f = []
        for i in range(n_chunks):
            state_f.append(state.astype(BF16))
            state = state * cd_f + kv[i][:RET_QK_DIM]
        state = jnp.zeros((RET_QK_DIM, RET_V_DIM), F32)
        state_b = [None] * n_chunks
        for i in reversed(range(n_chunks)):
            state_b[i] = state.astype(BF16)
            state = state * cd_b + kv[i][RET_QK_DIM:]
        for i in range(n_chunks):
            q = q_ref[hd, 0, rows(i), :]
            qf = q.astype(F32)
            q_scaled = jnp.concatenate([qf * xi_f, qf * xi_b], axis=1).astype(BF16)
            states = jnp.concatenate([state_f[i], state_b[i]], axis=0)
            scores = jnp.dot(q, kt_ref[hd, :, rows(i)], preferred_element_type=F32) * decay_mask
            o = (jnp.dot(scores.astype(BF16), v_ref[hd, 0, rows(i), :], preferred_element_type=F32)
                 + jnp.dot(q_scaled, states, preferred_element_type=F32))
            ms = jnp.mean(o * o, axis=-1, keepdims=True)
            gate = gate_ref[hd, 0, rows(i), :].astype(F32)
            o_ref[hd, 0, rows(i), :] = (gate * (o * lax.rsqrt(ms + EPS))).astype(BF16)


def _retention_call(dec, q, kt, v, gate, batch, s_len):
    hp = RET_HEADS_PER_STEP

    def blk(width):
        return pl.BlockSpec((hp, 1, s_len, width), lambda b, h: (h, b, 0, 0))

    t = batch * s_len
    r4 = lambda a: a.reshape(RET_HEADS, batch, s_len, a.shape[-1])
    out = pl.pallas_call(
        _retention_kernel,
        grid=(batch, RET_HEADS // hp),
        in_specs=[pl.BlockSpec((hp, 2, RET_V_DIM), lambda b, h: (h, 0, 0)),
                  blk(RET_QK_DIM), pl.BlockSpec((hp, RET_QK_DIM, s_len), lambda b, h: (h, 0, b)),
                  blk(RET_V_DIM), blk(RET_V_DIM)],
        out_specs=blk(RET_V_DIM),
        out_shape=jax.ShapeDtypeStruct((RET_HEADS, batch, s_len, RET_V_DIM), BF16),
        compiler_params=_params(("parallel", "parallel")),
        name="retention",
    )(dec, r4(q), kt, r4(v), r4(gate))
    return out.reshape(RET_HEADS, t, RET_V_DIM)


def _diff_attn_kernel(flag_ref, lam_ref, qt_ref, k_ref, vt_ref, band_ref, g_ref, o_ref, *, lambda_init):
    head = pl.program_id(0)
    s_len = k_ref.shape[2]
    n_kt = s_len // KEY_TILE
    n_qt = s_len // Q_TILE
    sub = KEY_CHUNK // KEY_TILE

    def learned_lambda():
        lam_vec = lam_ref[...]
        return (jnp.exp(jnp.sum(lam_vec[0:1] * lam_vec[1:2], keepdims=True))
                - jnp.exp(jnp.sum(lam_vec[2:3] * lam_vec[3:4], keepdims=True)) + lambda_init)

    def query_maps(j, u):
        start = j * s_len + u * Q_TILE
        cols = pl.ds(start if isinstance(u, int) else pl.multiple_of(start, Q_TILE), Q_TILE)
        qt = qt_ref[0, :, cols]
        in_map0 = lax.broadcasted_iota(jnp.int32, qt.shape, 0) < DIFF_SUB_DIM
        zeros = jnp.zeros_like(qt)
        return jnp.concatenate([jnp.where(in_map0, qt, zeros), jnp.where(in_map0, zeros, qt)], axis=1)

    def bias_tile(t, qi):
        b = band_ref[0, jnp.clip(t - qi, -BAND, BAND) + BAND]
        return jnp.concatenate([b, b], axis=1)

    def finish(j, u, acc, denom, lam):
        o2 = acc * (1.0 / denom)
        o_t = o2[:, :Q_TILE] - lam * o2[:, Q_TILE:]
        ms = jnp.mean(o_t * o_t, axis=0, keepdims=True)
        out_t = o_t * lax.rsqrt(ms + EPS) * g_ref[...] * (1.0 - lambda_init)
        start = u * Q_TILE
        rows = pl.ds(start if isinstance(u, int) else pl.multiple_of(start, Q_TILE), Q_TILE)
        o_ref[0, j, rows, :] = out_t.T.astype(BF16)

    @pl.when(flag_ref[head] == 0)
    def _bounded():
        lam = learned_lambda()
        for j in range(SEQ_PER_STEP):
            for u in range(n_qt):
                q_maps = query_maps(j, u)
                acc = jnp.zeros((DIFF_V_DIM, 2 * Q_TILE), F32)
                denom = jnp.zeros((8, 2 * Q_TILE), F32)
                for c in range(s_len // KEY_CHUNK):
                    rows = slice(c * KEY_CHUNK, (c + 1) * KEY_CHUNK)
                    cols = slice(j * s_len + c * KEY_CHUNK, j * s_len + (c + 1) * KEY_CHUNK)
                    s = jnp.dot(k_ref[0, j, rows, :], q_maps, preferred_element_type=F32)
                    bias = jnp.concatenate([bias_tile(c * sub + jj, u) for jj in range(sub)], axis=0)
                    e = jnp.exp2(s + bias)
                    denom = denom + jnp.sum(e.reshape(KEY_CHUNK // 8, 8, 2 * Q_TILE), axis=0)
                    acc = acc + jnp.dot(vt_ref[0, :, cols], e.astype(BF16), preferred_element_type=F32)
                finish(j, u, acc, jnp.sum(denom, axis=0, keepdims=True), lam)

    @pl.when(flag_ref[head] != 0)
    def _exact_maxima():
        lam = learned_lambda()
        for j in range(SEQ_PER_STEP):
            def tile_body(u, carry):
                q_maps = query_maps(j, u)

                def logits(t):
                    rows = pl.ds(pl.multiple_of(t * KEY_TILE, KEY_TILE), KEY_TILE)
                    return jnp.dot(k_ref[0, j, rows, :], q_maps, preferred_element_type=F32) + bias_tile(t, u)

                def max_body(t, m):
                    return jnp.maximum(m, jnp.max(logits(t), axis=0, keepdims=True))

                m = lax.fori_loop(0, n_kt, max_body, jnp.full((1, 2 * Q_TILE), jnp.finfo(F32).min, F32))

                def sum_body(t, inner):
                    acc, denom = inner
                    e = jnp.exp2(logits(t) - m)
                    cols = pl.ds(pl.multiple_of(j * s_len + t * KEY_TILE, KEY_TILE), KEY_TILE)
                    acc = acc + jnp.dot(vt_ref[0, :, cols], e.astype(BF16), preferred_element_type=F32)
                    return acc, denom + jnp.sum(e, axis=0, keepdims=True)

                acc, denom = lax.fori_loop(
                    0, n_kt, sum_body,
                    (jnp.zeros((DIFF_V_DIM, 2 * Q_TILE), F32), jnp.zeros((1, 2 * Q_TILE), F32)))
                finish(j, u, acc, denom, lam)
                return carry

            lax.fori_loop(0, n_qt, tile_body, 0)


def _diff_attn_call(flags, lam_vec, qt, k, vt, band, g_bcast, batch, s_len, lambda_init):
    t = batch * s_len
    sp = SEQ_PER_STEP
    head_t = pl.BlockSpec((1, DIFF_V_DIM, sp * s_len), lambda h, b: (h, 0, b))
    head_r = pl.BlockSpec((1, sp, s_len, DIFF_V_DIM), lambda h, b: (h, b, 0, 0))
    out = pl.pallas_call(
        functools.partial(_diff_attn_kernel, lambda_init=lambda_init),
        grid=(DIFF_HEADS, batch // sp),
        in_specs=[pl.BlockSpec(memory_space=pltpu.SMEM), _resident((4, DIFF_SUB_DIM)),
                  head_t, head_r, head_t,
                  pl.BlockSpec((1, N_BAND, KEY_TILE, Q_TILE), lambda h, b: (h, 0, 0, 0)),
                  _resident((DIFF_V_DIM, Q_TILE))],
        out_specs=head_r,
        out_shape=jax.ShapeDtypeStruct((DIFF_HEADS, batch, s_len, DIFF_V_DIM), BF16),
        compiler_params=_params(("parallel", "parallel")),
        name="diff_attn",
    )(flags, lam_vec, qt, k.reshape(DIFF_HEADS, batch, s_len, DIFF_V_DIM), vt, band, g_bcast)
    return out.reshape(DIFF_HEADS, t, DIFF_V_DIM)


def _output_kernel(x_ref, ret_ref, dif_ref, mg_ref, wo_ref, gf_ref, wg_ref, wu_ref, wd_ref, o_ref):
    ret = jnp.concatenate([ret_ref[h] for h in range(RET_HEADS)], axis=1).astype(F32)
    dif = jnp.concatenate([dif_ref[h] for h in range(DIFF_HEADS)], axis=1).astype(F32)
    mg = mg_ref[...].astype(F32)
    merged = (mg[:, :D_MODEL] * ret + mg[:, D_MODEL:] * dif).astype(BF16)
    x1 = x_ref[...] + jnp.dot(merged, wo_ref[...], preferred_element_type=F32)
    ms = jnp.mean(x1 * x1, axis=-1, keepdims=True)
    h = (x1 * lax.rsqrt(ms + EPS) * gf_ref[...]).astype(BF16)
    acc = x1
    ff = D_FF // FF_SPLIT
    for part in range(FF_SPLIT):
        cols = slice(part * ff, (part + 1) * ff)
        gate = jnp.dot(h, wg_ref[:, cols], preferred_element_type=F32)
        up = jnp.dot(h, wu_ref[:, cols], preferred_element_type=F32)
        act = (gate * _sigmoid(gate) * up).astype(BF16)
        acc = acc + jnp.dot(act, wd_ref[cols, :], preferred_element_type=F32)
    o_ref[...] = acc


def _output_call(x2, ret, dif, mg, wo, gf, wg, wu, wd):
    t, d = x2.shape
    tm = OUT_TM
    return pl.pallas_call(
        _output_kernel,
        grid=(t // tm,),
        in_specs=[pl.BlockSpec((tm, d), lambda i: (i, 0)),
                  pl.BlockSpec((RET_HEADS, tm, RET_V_DIM), lambda i: (0, i, 0)),
                  pl.BlockSpec((DIFF_HEADS, tm, DIFF_V_DIM), lambda i: (0, i, 0)),
                  pl.BlockSpec((tm, 2 * d), lambda i: (i, 0)),
                  _resident((d, d)), _resident((1, d)),
                  _resident((d, D_FF)), _resident((d, D_FF)), _resident((D_FF, d))],
        out_specs=pl.BlockSpec((tm, d), lambda i: (i, 0)),
        out_shape=jax.ShapeDtypeStruct((t, d), F32),
        compiler_params=_params(("parallel",)),
        name="merge_ffn",
    )(x2, ret, dif, mg, wo, gf, wg, wu, wd)


def _rotary_tables(s_len):
    half = RET_QK_DIM // 2
    inv = ROPE_BASE ** (-jnp.arange(half, dtype=F32) / half)
    ang = jnp.arange(s_len, dtype=F32)[:, None] * inv[None, :]
    cos, sin = jnp.cos(ang), jnp.sin(ang)
    return jnp.concatenate([cos, cos], axis=-1), jnp.concatenate([-sin, sin], axis=-1)


def _layer_weights(layer, norm_mix_g, w_in, ret_decay_fwd, ret_decay_bwd, q_norm_g, k_norm_g,
                   lam_q1, lam_k1, lam_q2, lam_k2, subln_g, w_out, norm_ffn_g, w_gate, w_up, w_down):
    blk = jnp.arange(MXU_DIM) // DIFF_SUB_DIM
    avg = jnp.where(blk[:, None] == blk[None, :], 1.0 / DIFF_SUB_DIM, 0.0).astype(BF16)
    dec = jnp.stack([ret_decay_fwd[layer], ret_decay_bwd[layer]], axis=1).astype(F32)
    return dict(
        w_in=w_in[layer].astype(BF16),
        norm_mix_g=norm_mix_g[layer].reshape(1, D_MODEL).astype(F32),
        avg=avg,
        q_gain=jnp.broadcast_to(
            (jnp.tile(q_norm_g[layer].astype(F32), 2) * (DIFF_SUB_DIM ** -0.5 * LOG2E))[:, None],
            (DIFF_V_DIM, PROJ_TM // PROJ_ROW_SPLIT)),
        k_gain=jnp.tile(k_norm_g[layer].astype(F32), D_MODEL // DIFF_SUB_DIM).reshape(1, D_MODEL),
        dec=jnp.broadcast_to(dec[:, :, None], (RET_HEADS, 2, RET_V_DIM)),
        q_norm_g=q_norm_g[layer].astype(F32).reshape(1, DIFF_SUB_DIM),
        k_norm_g=k_norm_g[layer].astype(F32).reshape(1, DIFF_SUB_DIM),
        lam_vec=jnp.stack([lam_q1[layer], lam_k1[layer], lam_q2[layer], lam_k2[layer]]).astype(F32),
        subln=jnp.broadcast_to(subln_g[layer].astype(F32)[:, None], (DIFF_V_DIM, Q_TILE)),
        w_out=w_out[layer].astype(BF16),
        norm_ffn_g=norm_ffn_g[layer].reshape(1, D_MODEL).astype(F32),
        w_gate=w_gate[layer].astype(BF16), w_up=w_up[layer].astype(BF16),
        w_down=w_down[layer].astype(BF16),
    )


def _encoder_layer(x, layer, lw, bias, cos, sin):
    batch, s_len, d = x.shape
    t = batch * s_len
    assert d == D_MODEL and s_len % PROJ_TM == 0 and s_len % KEY_CHUNK == 0 and s_len % RET_CHUNK == 0
    assert t % OUT_TM == 0 and batch % SEQ_PER_STEP == 0 and RET_HEADS % RET_HEADS_PER_STEP == 0
    x2 = x.reshape(t, d)
    rq, rkt, rv, rg, dqt, dk, dvt, mg = _in_proj_call(
        x2, lw["norm_mix_g"], lw["w_in"], cos, sin, lw["avg"], lw["q_gain"], lw["k_gain"], s_len)
    ret_out = _retention_call(lw["dec"], rq, rkt, rv, rg, batch, s_len)
    lambda_init = 0.8 - 0.6 * math.exp(-0.3 * layer)
    band, flags = bias
    dif_out = _diff_attn_call(flags, lw["lam_vec"], dqt, dk, dvt, band, lw["subln"], batch, s_len,
                              lambda_init)
    y2 = _output_call(x2, ret_out, dif_out, mg, lw["w_out"], lw["norm_ffn_g"],
                      lw["w_gate"], lw["w_up"], lw["w_down"])
    return y2.reshape(batch, s_len, d)


def kernel(x_prompt, x_sample, rel_bias_table, norm_mix_g, w_in, ret_decay_fwd, ret_decay_bwd,
           q_norm_g, k_norm_g, lam_q1, lam_k1, lam_q2, lam_k2, subln_g, w_out, norm_ffn_g,
           w_gate, w_up, w_down):
    table = rel_bias_table.astype(F32)
    layers = [_layer_weights(l, norm_mix_g, w_in, ret_decay_fwd, ret_decay_bwd, q_norm_g, k_norm_g,
                             lam_q1, lam_k1, lam_q2, lam_k2, subln_g, w_out, norm_ffn_g,
                             w_gate, w_up, w_down) for l in range(DEPTH)]
    biases = []
    for lw in layers:
        band, stat = _bias_call(table, lw["q_norm_g"], lw["k_norm_g"])
        biases.append((band, (stat[:, 0, 0] > 0.0).astype(jnp.int32)))
    outs = []
    for x in (x_prompt, x_sample):
        cos, sin = _rotary_tables(x.shape[1])
        for l in range(DEPTH):
            x = _encoder_layer(x, l, layers[l], biases[l], cos, sin)
        outs.append(x)
    return tuple(outs)
```

```python
import functools
import math

import jax
import jax.numpy as jnp
from jax import lax
from jax.experimental import pallas as pl
from jax.experimental.pallas import tpu as pltpu

D_MODEL = 1024
DEPTH = 1
RET_HEADS = 4
RET_V_DIM = D_MODEL // RET_HEADS
RET_QK_DIM = RET_V_DIM // 2
ROPE_BASE = 10000.0
DIFF_HEADS = 8
DIFF_V_DIM = D_MODEL // DIFF_HEADS
DIFF_SUB_DIM = DIFF_V_DIM // 2
N_BUCKETS = 32
MAX_DISTANCE = 128
D_FF = ((8 * D_MODEL // 3 + 255) // 256) * 256
EPS = 1e-6

LANES = 128
MXU_DIM = 256
VMEM_LIMIT_BYTES = 56 * 1024 * 1024

RET_CHUNK = 256
RET_HEADS_PER_STEP = 4
KEY_TILE = MXU_DIM
Q_TILE = MXU_DIM
KEY_CHUNK = 1024
SEQ_PER_STEP = 2
REACH_SLACK = 1.02
BAND = 2
N_BAND = 2 * BAND + 1
LOG2E = math.log2(math.e)
SHIFT_BOUND_LIMIT = 80.0
PROJ_TM = 512
PROJ_ROW_SPLIT = 2
OUT_TM = 512
FF_SPLIT = 11

assert (BAND - 1) * KEY_TILE + 1 >= MAX_DISTANCE, "bias tiles outside the band must be constant"
assert KEY_TILE == Q_TILE
assert D_FF % (FF_SPLIT * LANES) == 0

F32 = jnp.float32
BF16 = jnp.bfloat16


def _params(semantics):
    return pltpu.CompilerParams(dimension_semantics=semantics, vmem_limit_bytes=VMEM_LIMIT_BYTES)


def _resident(shape):
    return pl.BlockSpec(shape, lambda *_: (0,) * len(shape), pipeline_mode=pl.Buffered(1))


def _sigmoid(y):
    return 1.0 / (1.0 + jnp.exp(-y))


_COL_SIZES = (RET_HEADS * RET_QK_DIM, RET_HEADS * RET_QK_DIM, RET_HEADS * RET_V_DIM,
              RET_HEADS * RET_V_DIM, DIFF_HEADS * DIFF_V_DIM, DIFF_HEADS * DIFF_V_DIM,
              DIFF_HEADS * DIFF_V_DIM, 2 * D_MODEL)
COL_RQ, COL_RK, COL_RV, COL_RG, COL_DQ, COL_DK, COL_DV, COL_MG = (
    sum(_COL_SIZES[:i]) for i in range(len(_COL_SIZES)))
IN_COLS = sum(_COL_SIZES)


def _in_proj_kernel(xa_ref, xb_ref, g_ref, w_ref, cos_ref, sin_ref, avg_ref, gqt_ref, gk_ref,
                    rq_ref, rkt_ref, rv_ref, rg_ref, dqt_ref, dk_ref, dvt_ref, mg_ref, *, n_first):
    n_rows = xa_ref.shape[0] // PROJ_ROW_SPLIT
    half = RET_QK_DIM // 2
    from_first = pl.program_id(0) < n_first
    for r in range(PROJ_ROW_SPLIT):
        rs = slice(r * n_rows, (r + 1) * n_rows)
        x = jnp.where(from_first, xa_ref[rs, :], xb_ref[rs, :])
        ms = jnp.mean(x * x, axis=-1, keepdims=True)
        h = (x * lax.rsqrt(ms + EPS) * g_ref[...]).astype(BF16)

        def project(start, size):
            return jnp.dot(h, w_ref[:, start:start + size], preferred_element_type=F32)

        def group_rms(y, gain):
            yy = (y * y).astype(BF16)
            avg = avg_ref[...]
            ms = jnp.concatenate(
                [jnp.dot(yy[:, j * MXU_DIM:(j + 1) * MXU_DIM], avg, preferred_element_type=F32)
                 for j in range(y.shape[1] // MXU_DIM)], axis=1)
            return y * lax.rsqrt(ms + EPS) * gain

        cos = cos_ref[rs, :]
        sin = sin_ref[rs, :]
        y = project(COL_RQ, 2 * RET_HEADS * RET_QK_DIM)
        for hd in range(RET_HEADS):
            q = y[:, hd * RET_QK_DIM:(hd + 1) * RET_QK_DIM]
            k = y[:, COL_RK + hd * RET_QK_DIM:COL_RK + (hd + 1) * RET_QK_DIM]
            rq_ref[hd, rs, :] = (q * cos + pltpu.roll(q, half, 1) * sin).astype(BF16)
            rkt_ref[hd, :, rs] = ((k * cos + pltpu.roll(k, half, 1) * sin)
                                  * (RET_QK_DIM ** -0.5)).T.astype(BF16)
        y = project(COL_RG, RET_HEADS * RET_V_DIM)
        y = y * _sigmoid(y)
        for hd in range(RET_HEADS):
            rg_ref[hd, rs, :] = y[:, hd * RET_V_DIM:(hd + 1) * RET_V_DIM].astype(BF16)
        y = project(COL_DQ, DIFF_HEADS * DIFF_V_DIM)
        gq_t = gqt_ref[...]
        in_map0 = lax.broadcasted_iota(jnp.int32, gq_t.shape, 0) < DIFF_SUB_DIM
        for hd in range(DIFF_HEADS):
            yt = y[:, hd * DIFF_V_DIM:(hd + 1) * DIFF_V_DIM].T
            sq = yt * yt
            ms0 = jnp.sum(sq[:DIFF_SUB_DIM], axis=0, keepdims=True) * (1.0 / DIFF_SUB_DIM)
            ms1 = jnp.sum(sq[DIFF_SUB_DIM:], axis=0, keepdims=True) * (1.0 / DIFF_SUB_DIM)
            inv = jnp.where(in_map0, lax.rsqrt(ms0 + EPS), lax.rsqrt(ms1 + EPS))
            dqt_ref[hd, :, rs] = (yt * inv * gq_t).astype(BF16)
        y = group_rms(project(COL_DK, DIFF_HEADS * DIFF_V_DIM), gk_ref[...])
        for hd in range(DIFF_HEADS):
            dk_ref[hd, rs, :] = y[:, hd * DIFF_V_DIM:(hd + 1) * DIFF_V_DIM].astype(BF16)
        y = project(COL_DV, DIFF_HEADS * DIFF_V_DIM)
        for hd in range(DIFF_HEADS):
            dvt_ref[hd, :, rs] = y[:, hd * DIFF_V_DIM:(hd + 1) * DIFF_V_DIM].T.astype(BF16)
        for part in range(2):
            y = project(COL_MG + part * D_MODEL, D_MODEL)
            mg_ref[rs, part * D_MODEL:(part + 1) * D_MODEL] = _sigmoid(y).astype(BF16)
        y = project(COL_RV, RET_HEADS * RET_V_DIM)
        for hd in range(RET_HEADS):
            rv_ref[hd, rs, :] = y[:, hd * RET_V_DIM:(hd + 1) * RET_V_DIM].astype(BF16)


def _two_sources(n_first):
    return (lambda i: (jnp.minimum(i, n_first - 1), 0)), (lambda i: (jnp.maximum(i - n_first, 0), 0))


def _in_proj_call(xa, xb, g, w, cos, sin, avg, gq, gk, s_len):
    d = xa.shape[1]
    t = xa.shape[0] + xb.shape[0]
    tm = PROJ_TM
    n_first = xa.shape[0] // tm
    map_a, map_b = _two_sources(n_first)

    def rows_major(heads, width):
        return (jax.ShapeDtypeStruct((heads, t, width), BF16),
                pl.BlockSpec((heads, tm, width), lambda i: (0, i, 0)))

    def cols_major(heads, width):
        return (jax.ShapeDtypeStruct((heads, width, t), BF16),
                pl.BlockSpec((heads, width, tm), lambda i: (0, 0, i)))

    outs = [rows_major(RET_HEADS, RET_QK_DIM), cols_major(RET_HEADS, RET_QK_DIM),
            rows_major(RET_HEADS, RET_V_DIM), rows_major(RET_HEADS, RET_V_DIM),
            cols_major(DIFF_HEADS, DIFF_V_DIM), rows_major(DIFF_HEADS, DIFF_V_DIM),
            cols_major(DIFF_HEADS, DIFF_V_DIM),
            (jax.ShapeDtypeStruct((t, 2 * d), BF16), pl.BlockSpec((tm, 2 * d), lambda i: (i, 0)))]
    pos_spec = pl.BlockSpec((tm, RET_QK_DIM), lambda i: (i % (s_len // tm), 0))
    return pl.pallas_call(
        functools.partial(_in_proj_kernel, n_first=n_first),
        grid=(t // tm,),
        in_specs=[pl.BlockSpec((tm, d), map_a), pl.BlockSpec((tm, d), map_b),
                  _resident((1, d)), _resident((d, IN_COLS)),
                  pos_spec, pos_spec, _resident((MXU_DIM, MXU_DIM)),
                  _resident((DIFF_V_DIM, tm // PROJ_ROW_SPLIT)), _resident((1, d))],
        out_specs=[o[1] for o in outs],
        out_shape=[o[0] for o in outs],
        compiler_params=_params(("arbitrary",)),
        name="in_proj",
    )(xa, xb, g, w, cos, sin, avg, gq, gk)


def _bias_kernel(table_ref, gq_ref, gk_ref, o_ref, stat_ref):
    head = pl.program_id(0)
    nb = N_BUCKETS // 2
    max_exact = nb // 2
    j = lax.broadcasted_iota(jnp.int32, (KEY_TILE, Q_TILE), 0)
    i = lax.broadcasted_iota(jnp.int32, (KEY_TILE, Q_TILE), 1)
    tiles = []
    hi = lo = None
    for d in range(N_BAND):
        rel = (d - BAND) * KEY_TILE + j - i
        ret = jnp.where(rel > 0, nb, 0)
        n = jnp.abs(rel)
        nf = jnp.maximum(n, 1).astype(F32)
        large = max_exact + (jnp.log(nf / max_exact) / math.log(MAX_DISTANCE / max_exact)
                             * (nb - max_exact)).astype(jnp.int32)
        large = jnp.minimum(large, nb - 1)
        bucket = ret + jnp.where(n < max_exact, n, large)
        acc = jnp.zeros((KEY_TILE, Q_TILE), F32)
        for b in range(N_BUCKETS):
            acc = jnp.where(bucket == b, table_ref[b, head], acc)
        acc = acc * LOG2E
        tiles.append(acc)
        hi = acc if hi is None else jnp.maximum(hi, acc)
        lo = acc if lo is None else jnp.minimum(lo, acc)
    hi = jnp.max(jnp.max(hi, axis=0, keepdims=True), axis=1, keepdims=True)
    lo = jnp.min(jnp.min(lo, axis=0, keepdims=True), axis=1, keepdims=True)
    reach = (REACH_SLACK * LOG2E * DIFF_SUB_DIM ** 0.5
             * jnp.max(jnp.abs(gq_ref[...]), axis=1, keepdims=True)
             * jnp.max(jnp.abs(gk_ref[...]), axis=1, keepdims=True))
    shift = reach + hi
    for d in range(N_BAND):
        o_ref[0, d] = tiles[d] - shift
    spread = 2.0 * reach + hi - lo
    stat_ref[0] = jnp.broadcast_to(jnp.where(spread > SHIFT_BOUND_LIMIT, 1.0, 0.0), (8, LANES))


def _bias_call(table, gq, gk):
    return pl.pallas_call(
        _bias_kernel,
        grid=(DIFF_HEADS,),
        in_specs=[pl.BlockSpec(memory_space=pltpu.SMEM), _resident((1, DIFF_SUB_DIM)),
                  _resident((1, DIFF_SUB_DIM))],
        out_specs=[pl.BlockSpec((1, N_BAND, KEY_TILE, Q_TILE), lambda h: (h, 0, 0, 0)),
                   pl.BlockSpec((1, 8, LANES), lambda h: (h, 0, 0))],
        out_shape=[jax.ShapeDtypeStruct((DIFF_HEADS, N_BAND, KEY_TILE, Q_TILE), F32),
                   jax.ShapeDtypeStruct((DIFF_HEADS, 8, LANES), F32)],
        compiler_params=_params(("arbitrary",)),
        name="rel_bias",
    )(table, gq, gk)


def _retention_kernel(dec_ref, q_ref, kt_ref, v_ref, gate_ref, o_ref):
    c = RET_CHUNK
    s_len = q_ref.shape[2]
    n_chunks = s_len // c
    row_cc = lax.broadcasted_iota(jnp.int32, (c, c), 0)
    col_cc = lax.broadcasted_iota(jnp.int32, (c, c), 1)
    diff = (row_cc - col_cc).astype(F32)
    pos = lax.broadcasted_iota(jnp.int32, (1, c), 1).astype(F32)
    row_v = lax.broadcasted_iota(jnp.int32, (c, RET_V_DIM), 0).astype(F32)

    def rows(i):
        return slice(i * c, (i + 1) * c)

    for hd in range(RET_HEADS_PER_STEP):
        dec = dec_ref[hd]
        lg_f = jnp.log1p(-jnp.exp(dec[0:1]))
        lg_b = jnp.log1p(-jnp.exp(dec[1:2]))
        decay_mask = jnp.where(diff >= 0, jnp.exp(lg_f[:, :c] * diff), jnp.exp(lg_b[:, :c] * (-diff)))
        zeta_f = jnp.exp(lg_f[:, :c] * (c - 1.0 - pos))
        zeta_b = jnp.exp(lg_b[:, :c] * pos)
        row_q = row_v[:, :RET_QK_DIM]
        xi_f = jnp.exp(lg_f[:, :RET_QK_DIM] * (row_q + 1.0))
        xi_b = jnp.exp(lg_b[:, :RET_QK_DIM] * (c - row_q))
        cd_f = jnp.exp(lg_f * c)
        cd_b = jnp.exp(lg_b * c)

        kv = []
        for i in range(n_chunks):
            kt = kt_ref[hd, :, rows(i)].astype(F32)
            kz = jnp.concatenate([kt * zeta_f, kt * zeta_b], axis=0).astype(BF16)
            kv.append(jnp.dot(kz, v_ref[hd, 0, rows(i), :], preferred_element_type=F32))
        state = jnp.zeros((RET_QK_DIM, RET_V_DIM), F32)
        state_f = []
        for i in range(n_chunks):
            state_f.append(state.astype(BF16))
            state = state * cd_f + kv[i][:RET_QK_DIM]
        state = jnp.zeros((RET_QK_DIM, RET_V_DIM), F32)
        state_b = [None] * n_chunks
        for i in reversed(range(n_chunks)):
            state_b[i] = state.astype(BF16)
            state = state * cd_b + kv[i][RET_QK_DIM:]
        for i in range(n_chunks):
            q = q_ref[hd, 0, rows(i), :]
            qf = q.astype(F32)
            q_scaled = jnp.concatenate([qf * xi_f, qf * xi_b], axis=1).astype(BF16)
            states = jnp.concatenate([state_f[i], state_b[i]], axis=0)
            scores = jnp.dot(q, kt_ref[hd, :, rows(i)], preferred_element_type=F32) * decay_mask
            o = (jnp.dot(scores.astype(BF16), v_ref[hd, 0, rows(i), :], preferred_element_type=F32)
                 + jnp.dot(q_scaled, states, preferred_element_type=F32))
            ms = jnp.mean(o * o, axis=-1, keepdims=True)
            gate = gate_ref[hd, 0, rows(i), :].astype(F32)
            o_ref[hd, 0, rows(i), :] = (gate * (o * lax.rsqrt(ms + EPS))).astype(BF16)


def _retention_call(dec, q, kt, v, gate, batch, s_len):
    hp = RET_HEADS_PER_STEP

    def blk(width):
        return pl.BlockSpec((hp, 1, s_len, width), lambda b, h: (h, b, 0, 0))

    t = batch * s_len
    r4 = lambda a: a.reshape(RET_HEADS, batch, s_len, a.shape[-1])
    out = pl.pallas_call(
        _retention_kernel,
        grid=(batch, RET_HEADS // hp),
        in_specs=[pl.BlockSpec((hp, 2, RET_V_DIM), lambda b, h: (h, 0, 0)),
                  blk(RET_QK_DIM), pl.BlockSpec((hp, RET_QK_DIM, s_len), lambda b, h: (h, 0, b)),
                  blk(RET_V_DIM), blk(RET_V_DIM)],
        out_specs=blk(RET_V_DIM),
        out_shape=jax.ShapeDtypeStruct((RET_HEADS, batch, s_len, RET_V_DIM), BF16),
        compiler_params=_params(("parallel", "parallel")),
        name="retention",
    )(dec, r4(q), kt, r4(v), r4(gate))
    return out.reshape(RET_HEADS, t, RET_V_DIM)


def _diff_attn_kernel(flag_ref, lam_ref, qt_ref, k_ref, vt_ref, band_ref, g_ref, o_ref, *, lambda_init):
    head = pl.program_id(0)
    s_len = k_ref.shape[2]
    n_kt = s_len // KEY_TILE
    n_qt = s_len // Q_TILE
    sub = KEY_CHUNK // KEY_TILE

    def learned_lambda():
        lam_vec = lam_ref[...]
        return (jnp.exp(jnp.sum(lam_vec[0:1] * lam_vec[1:2], keepdims=True))
                - jnp.exp(jnp.sum(lam_vec[2:3] * lam_vec[3:4], keepdims=True)) + lambda_init)

    def query_maps(j, u):
        start = j * s_len + u * Q_TILE
        cols = pl.ds(start if isinstance(u, int) else pl.multiple_of(start, Q_TILE), Q_TILE)
        qt = qt_ref[0, :, cols]
        in_map0 = lax.broadcasted_iota(jnp.int32, qt.shape, 0) < DIFF_SUB_DIM
        zeros = jnp.zeros_like(qt)
        return jnp.concatenate([jnp.where(in_map0, qt, zeros), jnp.where(in_map0, zeros, qt)], axis=1)

    def bias_tile(t, qi):
        b = band_ref[0, jnp.clip(t - qi, -BAND, BAND) + BAND]
        return jnp.concatenate([b, b], axis=1)

    def finish(j, u, acc, denom, lam):
        o2 = acc * (1.0 / denom)
        o_t = o2[:, :Q_TILE] - lam * o2[:, Q_TILE:]
        ms = jnp.mean(o_t * o_t, axis=0, keepdims=True)
        out_t = o_t * lax.rsqrt(ms + EPS) * g_ref[...] * (1.0 - lambda_init)
        start = u * Q_TILE
        rows = pl.ds(start if isinstance(u, int) else pl.multiple_of(start, Q_TILE), Q_TILE)
        o_ref[0, j, rows, :] = out_t.T.astype(BF16)

    @pl.when(flag_ref[head] == 0)
    def _bounded():
        lam = learned_lambda()
        for j in range(SEQ_PER_STEP):
            for u in range(n_qt):
                q_maps = query_maps(j, u)
                acc = jnp.zeros((DIFF_V_DIM, 2 * Q_TILE), F32)
                denom = jnp.zeros((8, 2 * Q_TILE), F32)
                for c in range(s_len // KEY_CHUNK):
                    rows = slice(c * KEY_CHUNK, (c + 1) * KEY_CHUNK)
                    cols = slice(j * s_len + c * KEY_CHUNK, j * s_len + (c + 1) * KEY_CHUNK)
                    s = jnp.dot(k_ref[0, j, rows, :], q_maps, preferred_element_type=F32)
                    bias = jnp.concatenate([bias_tile(c * sub + jj, u) for jj in range(sub)], axis=0)
                    e = jnp.exp2(s + bias)
                    denom = denom + jnp.sum(e.reshape(KEY_CHUNK // 8, 8, 2 * Q_TILE), axis=0)
                    acc = acc + jnp.dot(vt_ref[0, :, cols], e.astype(BF16), preferred_element_type=F32)
                finish(j, u, acc, jnp.sum(denom, axis=0, keepdims=True), lam)

    @pl.when(flag_ref[head] != 0)
    def _exact_maxima():
        lam = learned_lambda()
        for j in range(SEQ_PER_STEP):
            def tile_body(u, carry):
                q_maps = query_maps(j, u)

                def logits(t):
                    rows = pl.ds(pl.multiple_of(t * KEY_TILE, KEY_TILE), KEY_TILE)
                    return jnp.dot(k_ref[0, j, rows, :], q_maps, preferred_element_type=F32) + bias_tile(t, u)

                def max_body(t, m):
                    return jnp.maximum(m, jnp.max(logits(t), axis=0, keepdims=True))

                m = lax.fori_loop(0, n_kt, max_body, jnp.full((1, 2 * Q_TILE), jnp.finfo(F32).min, F32))

                def sum_body(t, inner):
                    acc, denom = inner
                    e = jnp.exp2(logits(t) - m)
                    cols = pl.ds(pl.multiple_of(j * s_len + t * KEY_TILE, KEY_TILE), KEY_TILE)
                    acc = acc + jnp.dot(vt_ref[0, :, cols], e.astype(BF16), preferred_element_type=F32)
                    return acc, denom + jnp.sum(e, axis=0, keepdims=True)

                acc, denom = lax.fori_loop(
                    0, n_kt, sum_body,
                    (jnp.zeros((DIFF_V_DIM, 2 * Q_TILE), F32), jnp.zeros((1, 2 * Q_TILE), F32)))
                finish(j, u, acc, denom, lam)
                return carry

            lax.fori_loop(0, n_qt, tile_body, 0)


def _diff_attn_call(flags, lam_vec, qt, k, vt, band, g_bcast, batch, s_len, lambda_init):
    t = batch * s_len
    sp = SEQ_PER_STEP
    head_t = pl.BlockSpec((1, DIFF_V_DIM, sp * s_len), lambda h, b: (h, 0, b))
    head_r = pl.BlockSpec((1, sp, s_len, DIFF_V_DIM), lambda h, b: (h, b, 0, 0))
    out = pl.pallas_call(
        functools.partial(_diff_attn_kernel, lambda_init=lambda_init),
        grid=(DIFF_HEADS, batch // sp),
        in_specs=[pl.BlockSpec(memory_space=pltpu.SMEM), _resident((4, DIFF_SUB_DIM)),
                  head_t, head_r, head_t,
                  pl.BlockSpec((1, N_BAND, KEY_TILE, Q_TILE), lambda h, b: (h, 0, 0, 0)),
                  _resident((DIFF_V_DIM, Q_TILE))],
        out_specs=head_r,
        out_shape=jax.ShapeDtypeStruct((DIFF_HEADS, batch, s_len, DIFF_V_DIM), BF16),
        compiler_params=_params(("parallel", "parallel")),
        name="diff_attn",
    )(flags, lam_vec, qt, k.reshape(DIFF_HEADS, batch, s_len, DIFF_V_DIM), vt, band, g_bcast)
    return out.reshape(DIFF_HEADS, t, DIFF_V_DIM)


def _output_kernel(xa_ref, xb_ref, ret_ref, dif_ref, mg_ref, wo_ref, gf_ref, wg_ref, wu_ref, wd_ref,
                   oa_ref, ob_ref, *, n_first):
    from_first = pl.program_id(0) < n_first
    ret = jnp.concatenate([ret_ref[h] for h in range(RET_HEADS)], axis=1).astype(F32)
    dif = jnp.concatenate([dif_ref[h] for h in range(DIFF_HEADS)], axis=1).astype(F32)
    mg = mg_ref[...].astype(F32)
    merged = (mg[:, :D_MODEL] * ret + mg[:, D_MODEL:] * dif).astype(BF16)
    x = jnp.where(from_first, xa_ref[...], xb_ref[...])
    x1 = x + jnp.dot(merged, wo_ref[...], preferred_element_type=F32)
    ms = jnp.mean(x1 * x1, axis=-1, keepdims=True)
    h = (x1 * lax.rsqrt(ms + EPS) * gf_ref[...]).astype(BF16)
    acc = x1
    ff = D_FF // FF_SPLIT
    for part in range(FF_SPLIT):
        cols = slice(part * ff, (part + 1) * ff)
        gate = jnp.dot(h, wg_ref[:, cols], preferred_element_type=F32)
        up = jnp.dot(h, wu_ref[:, cols], preferred_element_type=F32)
        act = (gate * _sigmoid(gate) * up).astype(BF16)
        acc = acc + jnp.dot(act, wd_ref[cols, :], preferred_element_type=F32)

    @pl.when(from_first)
    def _():
        oa_ref[...] = acc

    @pl.when(jnp.logical_not(from_first))
    def _():
        ob_ref[...] = acc


def _output_call(xa, xb, ret, dif, mg, wo, gf, wg, wu, wd):
    d = xa.shape[1]
    t = xa.shape[0] + xb.shape[0]
    tm = OUT_TM
    n_first = xa.shape[0] // tm
    map_a, map_b = _two_sources(n_first)
    return pl.pallas_call(
        functools.partial(_output_kernel, n_first=n_first),
        grid=(t // tm,),
        in_specs=[pl.BlockSpec((tm, d), map_a), pl.BlockSpec((tm, d), map_b),
                  pl.BlockSpec((RET_HEADS, tm, RET_V_DIM), lambda i: (0, i, 0)),
                  pl.BlockSpec((DIFF_HEADS, tm, DIFF_V_DIM), lambda i: (0, i, 0)),
                  pl.BlockSpec((tm, 2 * d), lambda i: (i, 0)),
                  _resident((d, d)), _resident((1, d)),
                  _resident((d, D_FF)), _resident((d, D_FF)), _resident((D_FF, d))],
        out_specs=[pl.BlockSpec((tm, d), map_a), pl.BlockSpec((tm, d), map_b)],
        out_shape=[jax.ShapeDtypeStruct(xa.shape, F32), jax.ShapeDtypeStruct(xb.shape, F32)],
        compiler_params=_params(("arbitrary",)),
        name="merge_ffn",
    )(xa, xb, ret, dif, mg, wo, gf, wg, wu, wd)


def _rotary_tables(s_len):
    half = RET_QK_DIM // 2
    inv = ROPE_BASE ** (-jnp.arange(half, dtype=F32) / half)
    ang = jnp.arange(s_len, dtype=F32)[:, None] * inv[None, :]
    cos, sin = jnp.cos(ang), jnp.sin(ang)
    return jnp.concatenate([cos, cos], axis=-1), jnp.concatenate([-sin, sin], axis=-1)


def _layer_weights(layer, norm_mix_g, w_in, ret_decay_fwd, ret_decay_bwd, q_norm_g, k_norm_g,
                   lam_q1, lam_k1, lam_q2, lam_k2, subln_g, w_out, norm_ffn_g, w_gate, w_up, w_down):
    blk = jnp.arange(MXU_DIM) // DIFF_SUB_DIM
    avg = jnp.where(blk[:, None] == blk[None, :], 1.0 / DIFF_SUB_DIM, 0.0).astype(BF16)
    dec = jnp.stack([ret_decay_fwd[layer], ret_decay_bwd[layer]], axis=1).astype(F32)
    return dict(
        w_in=w_in[layer].astype(BF16),
        norm_mix_g=norm_mix_g[layer].reshape(1, D_MODEL).astype(F32),
        avg=avg,
        q_gain=jnp.broadcast_to(
            (jnp.tile(q_norm_g[layer].astype(F32), 2) * (DIFF_SUB_DIM ** -0.5 * LOG2E))[:, None],
            (DIFF_V_DIM, PROJ_TM // PROJ_ROW_SPLIT)),
        k_gain=jnp.tile(k_norm_g[layer].astype(F32), D_MODEL // DIFF_SUB_DIM).reshape(1, D_MODEL),
        dec=jnp.broadcast_to(dec[:, :, None], (RET_HEADS, 2, RET_V_DIM)),
        q_norm_g=q_norm_g[layer].astype(F32).reshape(1, DIFF_SUB_DIM),
        k_norm_g=k_norm_g[layer].astype(F32).reshape(1, DIFF_SUB_DIM),
        lam_vec=jnp.stack([lam_q1[layer], lam_k1[layer], lam_q2[layer], lam_k2[layer]]).astype(F32),
        subln=jnp.broadcast_to(subln_g[layer].astype(F32)[:, None], (DIFF_V_DIM, Q_TILE)),
        w_out=w_out[layer].astype(BF16),
        norm_ffn_g=norm_ffn_g[layer].reshape(1, D_MODEL).astype(F32),
        w_gate=w_gate[layer].astype(BF16), w_up=w_up[layer].astype(BF16),
        w_down=w_down[layer].astype(BF16),
    )


def _encoder_layer(xa, xb, layer, lw, bias, cos, sin):
    (batch_a, s_len, d), batch_b = xa.shape, xb.shape[0]
    batch = batch_a + batch_b
    assert xb.shape[1:] == (s_len, d) and d == D_MODEL
    assert s_len % PROJ_TM == 0 and s_len % OUT_TM == 0 and s_len % KEY_CHUNK == 0 and s_len % RET_CHUNK == 0
    assert batch % SEQ_PER_STEP == 0 and RET_HEADS % RET_HEADS_PER_STEP == 0
    xa2, xb2 = xa.reshape(batch_a * s_len, d), xb.reshape(batch_b * s_len, d)
    rq, rkt, rv, rg, dqt, dk, dvt, mg = _in_proj_call(
        xa2, xb2, lw["norm_mix_g"], lw["w_in"], cos, sin, lw["avg"], lw["q_gain"], lw["k_gain"], s_len)
    ret_out = _retention_call(lw["dec"], rq, rkt, rv, rg, batch, s_len)
    lambda_init = 0.8 - 0.6 * math.exp(-0.3 * layer)
    band, flags = bias
    dif_out = _diff_attn_call(flags, lw["lam_vec"], dqt, dk, dvt, band, lw["subln"], batch, s_len,
                              lambda_init)
    ya, yb = _output_call(xa2, xb2, ret_out, dif_out, mg, lw["w_out"], lw["norm_ffn_g"],
                          lw["w_gate"], lw["w_up"], lw["w_down"])
    return ya.reshape(xa.shape), yb.reshape(xb.shape)


def kernel(x_prompt, x_sample, rel_bias_table, norm_mix_g, w_in, ret_decay_fwd, ret_decay_bwd,
           q_norm_g, k_norm_g, lam_q1, lam_k1, lam_q2, lam_k2, subln_g, w_out, norm_ffn_g,
           w_gate, w_up, w_down):
    table = rel_bias_table.astype(F32)
    layers = [_layer_weights(l, norm_mix_g, w_in, ret_decay_fwd, ret_decay_bwd, q_norm_g, k_norm_g,
                             lam_q1, lam_k1, lam_q2, lam_k2, subln_g, w_out, norm_ffn_g,
                             w_gate, w_up, w_down) for l in range(DEPTH)]
    biases = []
    for lw in layers:
        band, stat = _bias_call(table, lw["q_norm_g"], lw["k_norm_g"])
        biases.append((band, (stat[:, 0, 0] > 0.0).astype(jnp.int32)))
    cos, sin = _rotary_tables(x_prompt.shape[1])
    xa, xb = x_prompt, x_sample
    for l in range(DEPTH):
        xa, xb = _encoder_layer(xa, xb, l, layers[l], biases[l], cos, sin)
    return (xa, xb)
```
